```python
import jax, jax.numpy as jnp
from jax import lax
import numpy as np

D_MODEL = 2048
BATCH = 4
SEQ = 4096
DEPTH = 4

CHUNK = 64
N_MIXERS = 2
POOL_WINDOWS = (2, 4, 8, 16)
N_POOL_GROUPS = len(POOL_WINDOWS)
POOL_GROUP = D_MODEL // N_POOL_GROUPS
RET_HEADS = 8
RET_QK_DIM = D_MODEL // RET_HEADS
RET_V_DIM = 2 * D_MODEL // RET_HEADS
RET_QK = RET_HEADS * RET_QK_DIM
RET_V = RET_HEADS * RET_V_DIM
ROPE_BASE = 10000.0
D_FF = ((8 * D_MODEL // 3 + 255) // 256) * 256
N_EXPERTS = 8
TOP_K = 2
MOE_BLOCK = 512
LN_EPS = 1e-5
GN_EPS = 1e-6
ALPHA = (2 * DEPTH) ** 0.25
BETA = (8 * DEPTH) ** -0.25
N_EVEN = (DEPTH + 1) // 2
N_ODD = DEPTH // 2

kernel_name = "hybrid_pool_retention_moe_deepnorm"


def layer_norm(x, gain, bias):
    xf = x.astype(jnp.float32)
    mu = jnp.mean(xf, axis=-1, keepdims=True)
    var = jnp.mean(jnp.square(xf - mu), axis=-1, keepdims=True)
    y = (xf - mu) * lax.rsqrt(var + LN_EPS)
    return (y * gain.astype(jnp.float32) + bias.astype(jnp.float32)).astype(x.dtype)


def pool_mixer(x, w, scale):
    B, S, D = x.shape
    cs = jnp.cumsum(x.astype(jnp.float32), axis=1)
    t = jnp.arange(1, S + 1, dtype=jnp.float32)[:, None]
    outs = []
    for g, win in enumerate(POOL_WINDOWS):
        c = cs[..., g * POOL_GROUP:(g + 1) * POOL_GROUP]
        lag = jnp.pad(c[:, :S - win], ((0, 0), (win, 0), (0, 0)))
        outs.append((c - lag) / jnp.minimum(t, float(win)))
    pooled = jnp.concatenate(outs, axis=-1).astype(x.dtype) - x
    y = jnp.einsum('bsgc,gcd->bsgd', pooled.reshape(B, S, N_POOL_GROUPS, POOL_GROUP), w)
    return y.reshape(B, S, D) * scale


def rope(a, pos):
    half = a.shape[-1] // 2
    inv_freq = ROPE_BASE ** (-jnp.arange(half, dtype=jnp.float32) / half)
    ang = pos[:, None] * inv_freq[None, :]
    cos = jnp.cos(ang)[None, :, None, :]
    sin = jnp.sin(ang)[None, :, None, :]
    a1, a2 = a[..., :half], a[..., half:]
    return jnp.concatenate([a1 * cos - a2 * sin, a1 * sin + a2 * cos], axis=-1)


def retention(x, w_in, w_o):
    B, S, D = x.shape
    H, dk, dv = RET_HEADS, RET_QK_DIM, RET_V_DIM
    nc = S // CHUNK
    proj = x @ w_in
    q, k, v, g = jnp.split(proj, [RET_QK, 2 * RET_QK, 2 * RET_QK + RET_V], axis=-1)
    pos = jnp.arange(S, dtype=jnp.float32)
    q = rope(q.reshape(B, S, H, dk).astype(jnp.float32), pos)
    k = rope(k.reshape(B, S, H, dk).astype(jnp.float32), pos) * (dk ** -0.5)
    v = v.reshape(B, S, H, dv).astype(jnp.float32)

    def to_chunks(a):
        return a.reshape(B, nc, CHUNK, H, a.shape[-1]).transpose(1, 0, 3, 2, 4)

    log_gamma = jnp.log1p(-jnp.exp2(-5.0 - jnp.arange(H, dtype=jnp.float32)))
    idx = jnp.arange(CHUNK, dtype=jnp.float32)
    intra = jnp.exp(jnp.abs(idx[:, None] - idx[None, :]) * log_gamma[:, None, None])
    q_decay = jnp.exp((idx + 1.0)[None, :] * log_gamma[:, None])[..., None]
    k_decay = jnp.exp((CHUNK - 1.0 - idx)[None, :] * log_gamma[:, None])[..., None]
    chunk_decay = jnp.exp(CHUNK * log_gamma)[:, None, None]

    def step(state, qkv):
        qc, kc, vc = qkv
        scores = jnp.einsum('bhid,bhjd->bhij', qc, kc) * intra
        out = (jnp.einsum('bhij,bhje->bhie', scores, vc)
               + jnp.einsum('bhid,bhde->bhie', qc * q_decay, state))
        state = chunk_decay * state + jnp.einsum('bhjd,bhje->bhde', kc * k_decay, vc)
        return state, out

    state0 = jnp.zeros((B, H, dk, dv), jnp.float32)
    _, o = lax.scan(step, state0, (to_chunks(q), to_chunks(k), to_chunks(v)))
    o = o.transpose(1, 0, 3, 2, 4).reshape(B, S, H, dv)
    mu = jnp.mean(o, axis=-1, keepdims=True)
    var = jnp.mean(jnp.square(o - mu), axis=-1, keepdims=True)
    o = ((o - mu) * lax.rsqrt(var + GN_EPS)).reshape(B, S, RET_V)
    o = (jax.nn.silu(g.astype(jnp.float32)) * o).astype(x.dtype)
    return o @ w_o


def swiglu(x, w_gate, w_up, w_down):
    return (jax.nn.silu(x @ w_gate) * (x @ w_up)) @ w_down


def moe_swiglu(x, w_router, w_gate, w_up, w_down):
    B, S, D = x.shape
    xt = x.reshape(-1, D)
    T = xt.shape[0]
    n_assign = T * TOP_K
    logits = (xt @ w_router).astype(jnp.float32)
    top_val, top_idx = lax.top_k(logits, TOP_K)
    gates = jax.nn.softmax(top_val, axis=-1)
    flat_e = top_idx.reshape(-1)
    order = jnp.argsort(flat_e)
    sorted_e = flat_e[order]
    tok = order // TOP_K
    counts = jnp.bincount(flat_e, length=N_EXPERTS)
    padded = (counts + MOE_BLOCK - 1) // MOE_BLOCK * MOE_BLOCK
    pad_end = jnp.cumsum(padded)
    pad_start = pad_end - padded
    start = jnp.cumsum(counts) - counts
    dest = pad_start[sorted_e] + jnp.arange(n_assign, dtype=jnp.int32) - start[sorted_e]
    n_blocks = (n_assign + MOE_BLOCK - 1) // MOE_BLOCK + N_EXPERTS
    buf = jnp.zeros((n_blocks * MOE_BLOCK, D), x.dtype).at[dest].set(xt[tok])
    block_start = jnp.arange(n_blocks, dtype=jnp.int32) * MOE_BLOCK
    block_expert = jnp.minimum(jnp.searchsorted(pad_end, block_start, side='right'), N_EXPERTS - 1)

    def expert_block(args):
        xb, e = args
        return swiglu(xb, w_gate[e], w_up[e], w_down[e])

    yb = lax.map(expert_block, (buf.reshape(n_blocks, MOE_BLOCK, D), block_expert))
    y_sorted = yb.reshape(-1, D)[dest] * gates.reshape(-1)[order][:, None].astype(x.dtype)
    y = jnp.zeros_like(xt).at[tok].add(y_sorted)
    return y.reshape(B, S, D)


def setup_inputs(seed: int = 0) -> dict:
    key = jax.random.key(seed)
    ks = jax.random.split(key, 15)
    nrm = jax.random.normal
    f32 = jnp.float32
    x = nrm(ks[0], (BATCH, SEQ, D_MODEL), f32)
    ln_gain = 1.0 + 0.02 * nrm(ks[1], (DEPTH, 2, D_MODEL), f32)
    ln_bias = 0.02 * nrm(ks[2], (DEPTH, 2, D_MODEL), f32)
    pool_w = nrm(ks[3], (N_EVEN, N_POOL_GROUPS, POOL_GROUP, POOL_GROUP), f32) * (POOL_GROUP ** -0.5 * BETA)
    pool_scale = 1.0 + 0.1 * nrm(ks[4], (N_EVEN, D_MODEL), f32)
    col_scale = jnp.concatenate([jnp.ones((2 * RET_QK,), f32), jnp.full((RET_V,), BETA, f32),
                                 jnp.ones((RET_V,), f32)])
    ret_w_in = nrm(ks[5], (N_ODD, D_MODEL, 2 * RET_QK + 2 * RET_V), f32) * (D_MODEL ** -0.5) * col_scale
    ret_w_o = nrm(ks[6], (N_ODD, RET_V, D_MODEL), f32) * (RET_V ** -0.5 * BETA)
    ffn_w_gate = nrm(ks[7], (N_EVEN, D_MODEL, D_FF), f32) * (D_MODEL ** -0.5 * BETA)
    ffn_w_up = nrm(ks[8], (N_EVEN, D_MODEL, D_FF), f32) * (D_MODEL ** -0.5 * BETA)
    ffn_w_down = nrm(ks[9], (N_EVEN, D_FF, D_MODEL), f32) * (D_FF ** -0.5 * BETA)
    moe_w_router = nrm(ks[10], (N_ODD, D_MODEL, N_EXPERTS), f32) * (D_MODEL ** -0.5)
    moe_w_gate = nrm(ks[11], (N_ODD, N_EXPERTS, D_MODEL, D_FF), f32) * (D_MODEL ** -0.5 * BETA)
    moe_w_up = nrm(ks[12], (N_ODD, N_EXPERTS, D_MODEL, D_FF), f32) * (D_MODEL ** -0.5 * BETA)
    moe_w_down = nrm(ks[13], (N_ODD, N_EXPERTS, D_FF, D_MODEL), f32) * (D_FF ** -0.5 * BETA)
    return {"x": x, "ln_gain": ln_gain, "ln_bias": ln_bias, "pool_w": pool_w,
            "pool_scale": pool_scale, "ret_w_in": ret_w_in, "ret_w_o": ret_w_o,
            "ffn_w_gate": ffn_w_gate, "ffn_w_up": ffn_w_up, "ffn_w_down": ffn_w_down,
            "moe_w_router": moe_w_router, "moe_w_gate": moe_w_gate, "moe_w_up": moe_w_up,
            "moe_w_down": moe_w_down}


def reference(x, ln_gain, ln_bias, pool_w, pool_scale, ret_w_in, ret_w_o,
              ffn_w_gate, ffn_w_up, ffn_w_down,
              moe_w_router, moe_w_gate, moe_w_up, moe_w_down):
    for i in range(DEPTH):
        j = i // N_MIXERS
        if i % N_MIXERS == 0:
            mix = pool_mixer(x, pool_w[j], pool_scale[j])
        else:
            mix = retention(x, ret_w_in[j], ret_w_o[j])
        x = layer_norm(ALPHA * x + mix, ln_gain[i, 0], ln_bias[i, 0])
        if i % 2 == 0:
            ch = swiglu(x, ffn_w_gate[j], ffn_w_up[j], ffn_w_down[j])
        else:
            ch = moe_swiglu(x, moe_w_router[j], moe_w_gate[j], moe_w_up[j], moe_w_down[j])
        x = layer_norm(ALPHA * x + ch, ln_gain[i, 1], ln_bias[i, 1])
    return x
```

```python
import functools
import math

import jax
import jax.numpy as jnp
from jax import lax
from jax.experimental import pallas as pl
from jax.experimental.pallas import tpu as pltpu

F32 = jnp.float32
BF16 = jnp.bfloat16

DEPTH = 4
POOL_WINDOWS = (2, 4, 8, 16)
RET_HEADS = 8
CHUNK = 64
ROPE_BASE = 10000.0
N_EXPERTS = 8
TOP_K = 2
LN_EPS = 1e-5
GN_EPS = 1e-6
ALPHA = (2 * DEPTH) ** 0.25
LOG_GAMMA = tuple(math.log1p(-(2.0 ** (-5 - h))) for h in range(RET_HEADS))

V7X_VMEM_BYTES = 64 * 1024 * 1024
V7X_F32_SUBLANES = 8
POOL_HALO = 2 * V7X_F32_SUBLANES

VMEM_LIMIT = V7X_VMEM_BYTES - 8 * 1024 * 1024
POOL_ROWS = 512
PROJ_ROWS, PROJ_COLS = 1024, 1024
RET_BLOCK = 256
OUTPROJ_ROWS, OUTPROJ_K = 512, 2048
FFN_ROWS, FFN_COLS = 512, 512
MOE_ROWS, MOE_COLS = 512, 512
ROUTE_ROWS = 512


def _params(*semantics):
    return pltpu.CompilerParams(dimension_semantics=semantics, vmem_limit_bytes=VMEM_LIMIT)


def _layer_norm(z, gain, bias):
    mu = jnp.mean(z, axis=-1, keepdims=True)
    zc = z - mu
    var = jnp.mean(zc * zc, axis=-1, keepdims=True)
    return zc * lax.rsqrt(var + LN_EPS) * gain + bias


def _row(ref, r):
    return ref[r:r + 1, :]


def _pool_ln_kernel(x_ref, halo_ref, w_ref, scale_ref, gain_ref, bias_ref, o_ref, *, layer, ln_row):
    ts, d = x_ref.shape[1], x_ref.shape[2]
    group = d // len(POOL_WINDOWS)
    s = pl.program_id(1)
    x = x_ref[0]
    halo = jnp.where(s > 0, halo_ref[0], 0.0)
    ext = jnp.concatenate([halo, x], axis=0)
    seen = (s * ts + 1 + lax.broadcasted_iota(jnp.int32, (ts, 1), 0)).astype(F32)
    mixed = []
    for g, win in enumerate(POOL_WINDOWS):
        cols = slice(g * group, (g + 1) * group)
        acc = ext[:, cols]
        span = 1
        while span < win:
            acc = acc + pltpu.roll(acc, span, 0)
            span *= 2
        pooled = acc[POOL_HALO:] * (1.0 / jnp.minimum(seen, float(win)))
        diff = (pooled - x[:, cols]).astype(BF16)
        mixed.append(jnp.dot(diff, w_ref[g].astype(BF16), preferred_element_type=F32))
    mix = jnp.concatenate(mixed, axis=-1) * _row(scale_ref, layer)
    o_ref[0] = _layer_norm(ALPHA * x + mix, _row(gain_ref, ln_row), _row(bias_ref, ln_row))


def _pool_ln(x, pool_w, pool_scale, gains, biases, *, layer, ln_row):
    b, s, d = x.shape
    ts = min(POOL_ROWS, s)
    assert s % ts == 0 and ts % POOL_HALO == 0 and max(POOL_WINDOWS) <= POOL_HALO
    halo_blocks = ts // POOL_HALO
    n_groups, group = pool_w.shape[1], pool_w.shape[2]
    full = lambda a: pl.BlockSpec(a.shape, lambda bi, si: (0,) * a.ndim)
    return pl.pallas_call(
        functools.partial(_pool_ln_kernel, layer=layer, ln_row=ln_row),
        grid=(b, s // ts),
        in_specs=[
            pl.BlockSpec((1, ts, d), lambda bi, si: (bi, si, 0)),
            pl.BlockSpec((1, POOL_HALO, d), lambda bi, si: (bi, jnp.maximum(si * halo_blocks - 1, 0), 0)),
            pl.BlockSpec((None, n_groups, group, group), lambda bi, si: (layer, 0, 0, 0)),
            full(pool_scale), full(gains), full(biases),
        ],
        out_specs=pl.BlockSpec((1, ts, d), lambda bi, si: (bi, si, 0)),
        out_shape=jax.ShapeDtypeStruct(x.shape, F32),
        compiler_params=_params("parallel", "parallel"),
        name="pool_ln",
    )(x, x, pool_w, pool_scale, gains, biases)


def _swiglu_partial(xb, wg_ref, wu_ref, wd_ref):
    gate = jnp.dot(xb, wg_ref[...].astype(BF16), preferred_element_type=F32)
    up = jnp.dot(xb, wu_ref[...].astype(BF16), preferred_element_type=F32)
    hidden = (jax.nn.silu(gate) * up).astype(BF16)
    return jnp.dot(hidden, wd_ref[...].astype(BF16), preferred_element_type=F32)


def _swiglu_ln_kernel(x_ref, wg_ref, wu_ref, wd_ref, gain_ref, bias_ref, o_ref, xb_ref, *, ln_row):
    f = pl.program_id(1)

    @pl.when(f == 0)
    def _():
        xb_ref[...] = x_ref[...].astype(BF16)

    part = _swiglu_partial(xb_ref[...], wg_ref, wu_ref, wd_ref)

    @pl.when(f == 0)
    def _():
        o_ref[...] = part

    @pl.when(f > 0)
    def _():
        o_ref[...] += part

    @pl.when(f == pl.num_programs(1) - 1)
    def _():
        o_ref[...] = _layer_norm(ALPHA * x_ref[...] + o_ref[...], _row(gain_ref, ln_row), _row(bias_ref, ln_row))


def _swiglu_ln(x, w_gate, w_up, w_down, gains, biases, *, layer, ln_row):
    t, d = x.shape
    d_ff = w_gate.shape[2]
    tm, tf = min(FFN_ROWS, t), min(FFN_COLS, d_ff)
    assert t % tm == 0 and d_ff % tf == 0
    full = lambda a: pl.BlockSpec(a.shape, lambda i, f: (0,) * a.ndim)
    return pl.pallas_call(
        functools.partial(_swiglu_ln_kernel, ln_row=ln_row),
        grid=(t // tm, d_ff // tf),
        in_specs=[
            pl.BlockSpec((tm, d), lambda i, f: (i, 0)),
            pl.BlockSpec((None, d, tf), lambda i, f: (layer, 0, f)),
            pl.BlockSpec((None, d, tf), lambda i, f: (layer, 0, f)),
            pl.BlockSpec((None, tf, d), lambda i, f: (layer, f, 0)),
            full(gains), full(biases),
        ],
        out_specs=pl.BlockSpec((tm, d), lambda i, f: (i, 0)),
        out_shape=jax.ShapeDtypeStruct((t, d), F32),
        scratch_shapes=[pltpu.VMEM((tm, d), BF16)],
        compiler_params=_params("parallel", "arbitrary"),
        name="swiglu_ln",
    )(x, w_gate, w_up, w_down, gains, biases)


def _proj_kernel(x_ref, w_ref, cos_ref, sin_ref, o_ref, xb_ref, *, rope_cols, k_cols_from, head_dim):
    j = pl.program_id(1)
    tn = o_ref.shape[1]

    @pl.when(j == 0)
    def _():
        xb_ref[...] = x_ref[...].astype(BF16)

    y = jnp.dot(xb_ref[...], w_ref[...].astype(BF16), preferred_element_type=F32)

    @pl.when(j * tn < rope_cols)
    def _():
        half = head_dim // 2
        cos, sin = cos_ref[...], sin_ref[...]
        scale = jnp.where(j * tn >= k_cols_from, head_dim ** -0.5, 1.0).astype(F32)
        parts = []
        for h in range(tn // head_dim):
            a1 = y[:, h * head_dim:h * head_dim + half]
            a2 = y[:, h * head_dim + half:(h + 1) * head_dim]
            parts.append((a1 * cos - a2 * sin) * scale)
            parts.append((a1 * sin + a2 * cos) * scale)
        o_ref[...] = jnp.concatenate(parts, axis=-1).astype(o_ref.dtype)

    @pl.when(j * tn >= rope_cols)
    def _():
        o_ref[...] = y.astype(o_ref.dtype)


def _project(x, w_in, cos, sin, *, layer, col_start, n_cols, rope_cols, head_dim, out_dtype):
    t, d = x.shape
    seq = cos.shape[0]
    tm, tn = min(PROJ_ROWS, seq), PROJ_COLS
    assert t % tm == 0 and seq % tm == 0 and n_cols % tn == 0 and col_start % tn == 0
    assert tn % head_dim == 0 and rope_cols % tn == 0 and cos.shape[1] == head_dim // 2
    col0 = col_start // tn
    seq_tiles = seq // tm
    return pl.pallas_call(
        functools.partial(_proj_kernel, rope_cols=rope_cols, k_cols_from=rope_cols // 2, head_dim=head_dim),
        grid=(t // tm, n_cols // tn),
        in_specs=[
            pl.BlockSpec((tm, d), lambda i, j: (i, 0)),
            pl.BlockSpec((None, d, tn), lambda i, j: (layer, 0, col0 + j)),
            pl.BlockSpec((tm, head_dim // 2), lambda i, j: (i % seq_tiles, 0)),
            pl.BlockSpec((tm, head_dim // 2), lambda i, j: (i % seq_tiles, 0)),
        ],
        out_specs=pl.BlockSpec((tm, tn), lambda i, j: (i, j)),
        out_shape=jax.ShapeDtypeStruct((t, n_cols), out_dtype),
        scratch_shapes=[pltpu.VMEM((tm, d), BF16)],
        compiler_params=_params("parallel", "arbitrary"),
        name="ret_proj",
    )(x, w_in, cos, sin)


def _retention_kernel(q_ref, k_ref, v_ref, g_ref, o_ref, state_ref):
    blk = pl.program_id(1)
    length = q_ref.shape[1]
    dk = q_ref.shape[2] // RET_HEADS
    dv = v_ref.shape[2] // RET_HEADS

    @pl.when(blk == 0)
    def _():
        state_ref[...] = jnp.zeros_like(state_ref)

    row = lax.broadcasted_iota(jnp.int32, (length, length), 0)
    col = lax.broadcasted_iota(jnp.int32, (length, length), 1)
    dist = jnp.abs(row - col).astype(F32)
    visible = (col // CHUNK) <= (row // CHUNK)
    pos = lax.broadcasted_iota(jnp.int32, (length, dk), 0).astype(F32)
    nt = (((1,), (1,)), ((), ()))
    tn = (((0,), (0,)), ((), ()))
    for h in range(RET_HEADS):
        lg = LOG_GAMMA[h]
        q = q_ref[0, :, h * dk:(h + 1) * dk]
        k = k_ref[0, :, h * dk:(h + 1) * dk]
        v = v_ref[0, :, h * dv:(h + 1) * dv]
        decay = jnp.where(visible, jnp.exp(dist * lg), 0.0)
        scores = lax.dot_general(q, k, nt, preferred_element_type=F32) * decay
        q_in = (q.astype(F32) * jnp.exp((pos + 1.0) * lg)).astype(BF16)
        k_out = (k.astype(F32) * jnp.exp((length - 1.0 - pos) * lg)).astype(BF16)
        state = state_ref[h]
        out = (jnp.dot(scores.astype(BF16), v, preferred_element_type=F32)
               + jnp.dot(q_in, state.astype(BF16), preferred_element_type=F32))
        state_ref[h] = math.exp(length * lg) * state + lax.dot_general(k_out, v, tn, preferred_element_type=F32)
        mu = jnp.mean(out, axis=-1, keepdims=True)
        oc = out - mu
        var = jnp.mean(oc * oc, axis=-1, keepdims=True)
        normed = oc * lax.rsqrt(var + GN_EPS)
        gate = g_ref[0, :, h * dv:(h + 1) * dv]
        o_ref[0, :, h * dv:(h + 1) * dv] = (jax.nn.silu(gate) * normed).astype(o_ref.dtype)


def _retention_core(qkv, gate, *, batch):
    t, v_dim = gate.shape
    qk_dim = (qkv.shape[1] - v_dim) // 2
    seq = t // batch
    length = min(RET_BLOCK, seq)
    assert seq % length == 0 and length % CHUNK == 0 and v_dim % qk_dim == 0
    qkv3 = qkv.reshape(batch, seq, qkv.shape[1])
    gate3 = gate.reshape(batch, seq, v_dim)
    v_block = 2 * qk_dim // v_dim
    assert v_block * v_dim == 2 * qk_dim
    out = pl.pallas_call(
        _retention_kernel,
        grid=(batch, seq // length),
        in_specs=[
            pl.BlockSpec((1, length, qk_dim), lambda b, s: (b, s, 0)),
            pl.BlockSpec((1, length, qk_dim), lambda b, s: (b, s, 1)),
            pl.BlockSpec((1, length, v_dim), lambda b, s: (b, s, v_block)),
            pl.BlockSpec((1, length, v_dim), lambda b, s: (b, s, 0)),
        ],
        out_specs=pl.BlockSpec((1, length, v_dim), lambda b, s: (b, s, 0)),
        out_shape=jax.ShapeDtypeStruct((batch, seq, v_dim), BF16),
        scratch_shapes=[pltpu.VMEM((RET_HEADS, qk_dim // RET_HEADS, v_dim // RET_HEADS), F32)],
        compiler_params=_params("parallel", "arbitrary"),
        name="ret_core",
    )(qkv3, qkv3, qkv3, gate3)
    return out.reshape(t, v_dim)


def _outproj_ln_router_kernel(o_ref, wo_ref, x_ref, gain_ref, bias_ref, wr_ref,
                              x1_ref, idx_ref, gates_ref, *, ln_row):
    kk = pl.program_id(1)
    part = jnp.dot(o_ref[...], wo_ref[...].astype(BF16), preferred_element_type=F32)

    @pl.when(kk == 0)
    def _():
        x1_ref[...] = part

    @pl.when(kk > 0)
    def _():
        x1_ref[...] += part

    @pl.when(kk == pl.num_programs(1) - 1)
    def _():
        x1 = _layer_norm(ALPHA * x_ref[...] + x1_ref[...], _row(gain_ref, ln_row), _row(bias_ref, ln_row))
        x1_ref[...] = x1
        x_hi = x1.astype(BF16)
        x_lo = (x1 - x_hi.astype(F32)).astype(BF16)
        w = wr_ref[...]
        w_hi = w.astype(BF16)
        w_lo = (w - w_hi.astype(F32)).astype(BF16)
        logits = (jnp.dot(x_hi, w_hi, preferred_element_type=F32)
                  + (jnp.dot(x_hi, w_lo, preferred_element_type=F32)
                     + jnp.dot(x_lo, w_hi, preferred_element_type=F32)))
        expert = lax.broadcasted_iota(jnp.int32, logits.shape, 1).astype(F32)
        none = float(N_EXPERTS)
        m1 = jnp.max(logits, axis=1, keepdims=True)
        i1 = jnp.min(jnp.where(logits == m1, expert, none), axis=1, keepdims=True)
        rest = jnp.where(expert == i1, -jnp.inf, logits)
        m2 = jnp.max(rest, axis=1, keepdims=True)
        i2 = jnp.min(jnp.where(rest == m2, expert, none), axis=1, keepdims=True)
        p2 = jnp.exp(m2 - m1)
        denom = 1.0 + p2
        first = lax.broadcasted_iota(jnp.int32, idx_ref.shape, 1) == 0
        idx_ref[...] = jnp.where(first, i1, i2).astype(jnp.int32)
        gates_ref[...] = jnp.where(first, 1.0 / denom, p2 / denom)


def _outproj_ln_router(o, w_o, x, gains, biases, w_router, *, layer, ln_row):
    t, d = x.shape
    v_dim = o.shape[1]
    tm, tk = min(OUTPROJ_ROWS, t), min(OUTPROJ_K, v_dim)
    assert t % tm == 0 and v_dim % tk == 0 and TOP_K == 2
    full = lambda a: pl.BlockSpec(a.shape, lambda i, k: (0,) * a.ndim)
    return pl.pallas_call(
        functools.partial(_outproj_ln_router_kernel, ln_row=ln_row),
        grid=(t // tm, v_dim // tk),
        in_specs=[
            pl.BlockSpec((tm, tk), lambda i, k: (i, k)),
            pl.BlockSpec((None, tk, d), lambda i, k: (layer, k, 0)),
            pl.BlockSpec((tm, d), lambda i, k: (i, 0)),
            full(gains), full(biases),
            pl.BlockSpec((None, d, N_EXPERTS), lambda i, k: (layer, 0, 0)),
        ],
        out_specs=[
            pl.BlockSpec((tm, d), lambda i, k: (i, 0)),
            pl.BlockSpec((tm, TOP_K), lambda i, k: (i, 0)),
            pl.BlockSpec((tm, TOP_K), lambda i, k: (i, 0)),
        ],
        out_shape=[
            jax.ShapeDtypeStruct((t, d), F32),
            jax.ShapeDtypeStruct((t, TOP_K), jnp.int32),
            jax.ShapeDtypeStruct((t, TOP_K), F32),
        ],
        compiler_params=_params("parallel", "arbitrary"),
        name="outproj_ln_router",
    )(o, w_o, x, gains, biases, w_router)


def _route_plan(idx, block_rows):
    flat_e = idx.reshape(-1)
    n_assign = flat_e.shape[0]
    experts = jnp.arange(N_EXPERTS, dtype=jnp.int32)
    onehot = (flat_e[:, None] == experts[None, :]).astype(jnp.int32)
    running = jnp.cumsum(onehot, axis=0)
    rank = jnp.sum((running - onehot) * onehot, axis=1)
    counts = running[-1]
    padded = (counts + block_rows - 1) // block_rows * block_rows
    pad_end = jnp.cumsum(padded)
    pad_start = pad_end - padded
    dest = (jnp.sum(onehot * pad_start[None, :], axis=1) + rank).astype(jnp.int32)
    n_blocks = n_assign // block_rows + N_EXPERTS
    block_start = jnp.arange(n_blocks, dtype=jnp.int32) * block_rows
    block_expert = jnp.minimum(jnp.sum(block_start[:, None] >= pad_end[None, :], axis=1), N_EXPERTS - 1)
    used_blocks = (pad_end[-1] // block_rows).reshape(1)
    return dest, block_expert.astype(jnp.int32), used_blocks.astype(jnp.int32), n_blocks


def _dispatch_kernel(dest_ref, x_hbm, init_hbm, xs_hbm, sem, *, tokens):
    del init_hbm
    base = pl.program_id(0) * tokens

    def row_copy(t, k):
        return pltpu.make_async_copy(x_hbm.at[pl.ds(t, 1)], xs_hbm.at[pl.ds(dest_ref[TOP_K * t + k], 1)], sem)

    def issue(j, carry):
        for k in range(TOP_K):
            row_copy(base + j, k).start()
        return carry

    def drain(j, carry):
        for k in range(TOP_K):
            row_copy(base + j, k).wait()
        return carry

    lax.fori_loop(0, tokens, issue, 0)
    lax.fori_loop(0, tokens, drain, 0)


def _dispatch(x, dest, n_rows):
    t, d = x.shape
    tokens = min(ROUTE_ROWS, t)
    assert t % tokens == 0
    return pl.pallas_call(
        functools.partial(_dispatch_kernel, tokens=tokens),
        grid_spec=pltpu.PrefetchScalarGridSpec(
            num_scalar_prefetch=1,
            grid=(t // tokens,),
            in_specs=[pl.BlockSpec(memory_space=pl.ANY), pl.BlockSpec(memory_space=pl.ANY)],
            out_specs=pl.BlockSpec(memory_space=pl.ANY),
            scratch_shapes=[pltpu.SemaphoreType.DMA(())],
        ),
        out_shape=jax.ShapeDtypeStruct((n_rows, d), x.dtype),
        input_output_aliases={2: 0},
        compiler_params=pltpu.CompilerParams(dimension_semantics=("arbitrary",), has_side_effects=True),
        name="moe_dispatch",
    )(dest, x, jnp.zeros((n_rows, d), x.dtype))


def _expert_kernel(block_expert_ref, used_ref, xs_ref, wg_ref, wu_ref, wd_ref, ys_ref, xb_ref):
    del block_expert_ref
    i, f = pl.program_id(0), pl.program_id(1)
    live = i < used_ref[0]

    @pl.when(jnp.logical_and(live, f == 0))
    def _():
        xb_ref[...] = xs_ref[...].astype(BF16)

    @pl.when(jnp.logical_and(live, f == 0))
    def _():
        ys_ref[...] = _swiglu_partial(xb_ref[...], wg_ref, wu_ref, wd_ref)

    @pl.when(jnp.logical_and(live, f > 0))
    def _():
        ys_ref[...] += _swiglu_partial(xb_ref[...], wg_ref, wu_ref, wd_ref)

    @pl.when(jnp.logical_and(jnp.logical_not(live), f == 0))
    def _():
        ys_ref[...] = jnp.zeros_like(ys_ref)


def _experts(xs, block_expert, used_blocks, w_gate, w_up, w_down, *, layer, block_rows):
    n_rows, d = xs.shape
    d_ff = w_gate.shape[3]
    tf = min(MOE_COLS, d_ff)
    assert n_rows % block_rows == 0 and d_ff % tf == 0
    n_f = d_ff // tf

    def col(i, f, used):
        return jnp.where(i < used[0], f, n_f - 1)

    return pl.pallas_call(
        _expert_kernel,
        grid_spec=pltpu.PrefetchScalarGridSpec(
            num_scalar_prefetch=2,
            grid=(n_rows // block_rows, n_f),
            in_specs=[
                pl.BlockSpec((block_rows, d), lambda i, f, be, used: (i, 0)),
                pl.BlockSpec((None, None, d, tf), lambda i, f, be, used: (layer, be[i], 0, col(i, f, used))),
                pl.BlockSpec((None, None, d, tf), lambda i, f, be, used: (layer, be[i], 0, col(i, f, used))),
                pl.BlockSpec((None, None, tf, d), lambda i, f, be, used: (layer, be[i], col(i, f, used), 0)),
            ],
            out_specs=pl.BlockSpec((block_rows, d), lambda i, f, be, used: (i, 0)),
            scratch_shapes=[pltpu.VMEM((block_rows, d), BF16)],
        ),
        out_shape=jax.ShapeDtypeStruct((n_rows, d), F32),
        compiler_params=_params("parallel", "arbitrary"),
        name="moe_experts",
    )(block_expert, used_blocks, xs, w_gate, w_up, w_down)


def _combine_ln_kernel(dest_ref, x_ref, gates_ref, gain_ref, bias_ref, ys_hbm, o_ref, y_buf, sem, *, ln_row):
    tokens = x_ref.shape[0]
    base = pl.program_id(0) * tokens

    def row_copy(j, k):
        return pltpu.make_async_copy(ys_hbm.at[pl.ds(dest_ref[TOP_K * (base + j) + k], 1)],
                                     y_buf.at[k, pl.ds(j, 1)], sem)

    def issue(j, carry):
        for k in range(TOP_K):
            row_copy(j, k).start()
        return carry

    def drain(j, carry):
        for k in range(TOP_K):
            row_copy(j, k).wait()
        return carry

    lax.fori_loop(0, tokens, issue, 0)
    lax.fori_loop(0, tokens, drain, 0)
    gates = gates_ref[...]
    mix = gates[:, 0:1] * y_buf[0] + gates[:, 1:2] * y_buf[1]
    o_ref[...] = _layer_norm(ALPHA * x_ref[...] + mix, _row(gain_ref, ln_row), _row(bias_ref, ln_row))


def _combine_ln(x, ys, dest, gates, gains, biases, *, ln_row):
    t, d = x.shape
    tokens = min(ROUTE_ROWS, t)
    assert t % tokens == 0
    full = lambda a: pl.BlockSpec(a.shape, lambda i, dest: (0,) * a.ndim)
    return pl.pallas_call(
        functools.partial(_combine_ln_kernel, ln_row=ln_row),
        grid_spec=pltpu.PrefetchScalarGridSpec(
            num_scalar_prefetch=1,
            grid=(t // tokens,),
            in_specs=[
                pl.BlockSpec((tokens, d), lambda i, dest: (i, 0)),
                pl.BlockSpec((tokens, TOP_K), lambda i, dest: (i, 0)),
                full(gains), full(biases),
                pl.BlockSpec(memory_space=pl.ANY),
            ],
            out_specs=pl.BlockSpec((tokens, d), lambda i, dest: (i, 0)),
            scratch_shapes=[pltpu.VMEM((TOP_K, tokens, d), F32), pltpu.SemaphoreType.DMA(())],
        ),
        out_shape=jax.ShapeDtypeStruct((t, d), F32),
        compiler_params=_params("arbitrary"),
        name="moe_combine_ln",
    )(dest, x, gates, gains, biases, ys)


def _moe_ln(x, idx, gates, w_gate, w_up, w_down, gains, biases, *, layer, ln_row):
    dest, block_expert, used_blocks, n_blocks = _route_plan(idx, MOE_ROWS)
    xs = _dispatch(x, dest, n_blocks * MOE_ROWS)
    ys = _experts(xs, block_expert, used_blocks, w_gate, w_up, w_down, layer=layer, block_rows=MOE_ROWS)
    return _combine_ln(x, ys, dest, gates, gains, biases, ln_row=ln_row)


def _rope_tables(seq, head_dim):
    half = head_dim // 2
    inv_freq = ROPE_BASE ** (-jnp.arange(half, dtype=F32) / half)
    ang = jnp.arange(seq, dtype=F32)[:, None] * inv_freq[None, :]
    return jnp.cos(ang), jnp.sin(ang)


def kernel(x, ln_gain, ln_bias, pool_w, pool_scale, ret_w_in, ret_w_o, ffn_w_gate, ffn_w_up, ffn_w_down,
           moe_w_router, moe_w_gate, moe_w_up, moe_w_down):
    batch, seq, d = x.shape
    t = batch * seq
    gains = ln_gain.reshape(2 * DEPTH, d)
    biases = ln_bias.reshape(2 * DEPTH, d)
    qk_dim = ret_w_o.shape[2]
    v_dim = ret_w_o.shape[1]
    head_dim = qk_dim // RET_HEADS
    cos, sin = _rope_tables(seq, head_dim)
    pool_w = pool_w.astype(BF16)
    ret_w_in = ret_w_in.astype(BF16)
    ret_w_o = ret_w_o.astype(BF16)
    ffn_w_gate, ffn_w_up, ffn_w_down = (w.astype(BF16) for w in (ffn_w_gate, ffn_w_up, ffn_w_down))

    for i in range(DEPTH):
        j = i // 2
        if i % 2 == 0:
            x = _pool_ln(x, pool_w, pool_scale, gains, biases, layer=j, ln_row=2 * i)
            x2 = _swiglu_ln(x.reshape(t, d), ffn_w_gate, ffn_w_up, ffn_w_down, gains, biases,
                            layer=j, ln_row=2 * i + 1)
        else:
            x2 = x.reshape(t, d)
            qkv = _project(x2, ret_w_in, cos, sin, layer=j, col_start=0, n_cols=2 * qk_dim + v_dim,
                           rope_cols=2 * qk_dim, head_dim=head_dim, out_dtype=BF16)
            gate = _project(x2, ret_w_in, cos, sin, layer=j, col_start=2 * qk_dim + v_dim, n_cols=v_dim,
                            rope_cols=0, head_dim=head_dim, out_dtype=F32)
            o = _retention_core(qkv, gate, batch=batch)
            x1, idx, gates = _outproj_ln_router(o, ret_w_o, x2, gains, biases, moe_w_router,
                                                layer=j, ln_row=2 * i)
            x2 = _moe_ln(x1, idx, gates, moe_w_gate, moe_w_up, moe_w_down, gains, biases,
                         layer=j, ln_row=2 * i + 1)
        x = x2.reshape(batch, seq, d)
    return x
```

```python
import functools
import math

import jax
import jax.numpy as jnp
from jax import lax
from jax.experimental import pallas as pl
from jax.experimental.pallas import tpu as pltpu

F32 = jnp.float32
BF16 = jnp.bfloat16

DEPTH = 4
POOL_WINDOWS = (2, 4, 8, 16)
RET_HEADS = 8
CHUNK = 64
ROPE_BASE = 10000.0
N_EXPERTS = 8
TOP_K = 2
LN_EPS = 1e-5
GN_EPS = 1e-6
ALPHA = (2 * DEPTH) ** 0.25
LOG_GAMMA = tuple(math.log1p(-(2.0 ** (-5 - h))) for h in range(RET_HEADS))

V7X_VMEM_BYTES = 64 * 1024 * 1024
V7X_F32_SUBLANES = 8
POOL_HALO = 2 * V7X_F32_SUBLANES

VMEM_LIMIT = V7X_VMEM_BYTES - 8 * 1024 * 1024
POOL_ROWS = 512
PROJ_ROWS, PROJ_COLS = 1024, 1024
RET_BLOCK = 256
OUTPROJ_ROWS, OUTPROJ_K = 512, 2048
FFN_ROWS, FFN_COLS = 512, 512
MOE_ROWS, MOE_COLS = 512, 512
ROUTE_ROWS = 512
DMA_UNROLL = 8


def _params(*semantics):
    return pltpu.CompilerParams(dimension_semantics=semantics, vmem_limit_bytes=VMEM_LIMIT)


def _layer_norm(z, gain, bias):
    mu = jnp.mean(z, axis=-1, keepdims=True)
    zc = z - mu
    var = jnp.mean(zc * zc, axis=-1, keepdims=True)
    return zc * lax.rsqrt(var + LN_EPS) * gain + bias


def _row(ref, r):
    return ref[r:r + 1, :]


def _pool_ln_kernel(x_ref, halo_ref, w_ref, scale_ref, gain_ref, bias_ref, o_ref, *, layer, ln_row):
    ts, d = x_ref.shape[1], x_ref.shape[2]
    group = d // len(POOL_WINDOWS)
    s = pl.program_id(1)
    x = x_ref[0]
    halo = jnp.where(s > 0, halo_ref[0], 0.0)
    ext = jnp.concatenate([halo, x], axis=0)
    seen = (s * ts + 1 + lax.broadcasted_iota(jnp.int32, (ts, 1), 0)).astype(F32)
    mixed = []
    for g, win in enumerate(POOL_WINDOWS):
        cols = slice(g * group, (g + 1) * group)
        acc = ext[:, cols]
        span = 1
        while span < win:
            acc = acc + pltpu.roll(acc, span, 0)
            span *= 2
        pooled = acc[POOL_HALO:] * (1.0 / jnp.minimum(seen, float(win)))
        diff = (pooled - x[:, cols]).astype(BF16)
        mixed.append(jnp.dot(diff, w_ref[g].astype(BF16), preferred_element_type=F32))
    mix = jnp.concatenate(mixed, axis=-1) * _row(scale_ref, layer)
    o_ref[0] = _layer_norm(ALPHA * x + mix, _row(gain_ref, ln_row), _row(bias_ref, ln_row))


def _pool_ln(x, pool_w, pool_scale, gains, biases, *, layer, ln_row):
    b, s, d = x.shape
    ts = min(POOL_ROWS, s)
    assert s % ts == 0 and ts % POOL_HALO == 0 and max(POOL_WINDOWS) <= POOL_HALO
    halo_blocks = ts // POOL_HALO
    n_groups, group = pool_w.shape[1], pool_w.shape[2]
    full = lambda a: pl.BlockSpec(a.shape, lambda bi, si: (0,) * a.ndim)
    return pl.pallas_call(
        functools.partial(_pool_ln_kernel, layer=layer, ln_row=ln_row),
        grid=(b, s // ts),
        in_specs=[
            pl.BlockSpec((1, ts, d), lambda bi, si: (bi, si, 0)),
            pl.BlockSpec((1, POOL_HALO, d), lambda bi, si: (bi, jnp.maximum(si * halo_blocks - 1, 0), 0)),
            pl.BlockSpec((None, n_groups, group, group), lambda bi, si: (layer, 0, 0, 0)),
            full(pool_scale), full(gains), full(biases),
        ],
        out_specs=pl.BlockSpec((1, ts, d), lambda bi, si: (bi, si, 0)),
        out_shape=jax.ShapeDtypeStruct(x.shape, F32),
        compiler_params=_params("parallel", "parallel"),
        name="pool_ln",
    )(x, x, pool_w, pool_scale, gains, biases)


def _swiglu_partial(xb, wg_ref, wu_ref, wd_ref):
    gate = jnp.dot(xb, wg_ref[...].astype(BF16), preferred_element_type=F32)
    up = jnp.dot(xb, wu_ref[...].astype(BF16), preferred_element_type=F32)
    hidden = (jax.nn.silu(gate) * up).astype(BF16)
    return jnp.dot(hidden, wd_ref[...].astype(BF16), preferred_element_type=F32)


def _swiglu_ln_kernel(x_ref, wg_ref, wu_ref, wd_ref, gain_ref, bias_ref, o_ref, xb_ref, *, ln_row):
    f = pl.program_id(1)

    @pl.when(f == 0)
    def _():
        xb_ref[...] = x_ref[...].astype(BF16)
        o_ref[...] = jnp.zeros_like(o_ref)

    o_ref[...] += _swiglu_partial(xb_ref[...], wg_ref, wu_ref, wd_ref)

    @pl.when(f == pl.num_programs(1) - 1)
    def _():
        o_ref[...] = _layer_norm(ALPHA * x_ref[...] + o_ref[...], _row(gain_ref, ln_row), _row(bias_ref, ln_row))


def _swiglu_ln(x, w_gate, w_up, w_down, gains, biases, *, layer, ln_row):
    t, d = x.shape
    d_ff = w_gate.shape[2]
    tm, tf = min(FFN_ROWS, t), min(FFN_COLS, d_ff)
    assert t % tm == 0 and d_ff % tf == 0
    full = lambda a: pl.BlockSpec(a.shape, lambda i, f: (0,) * a.ndim)
    return pl.pallas_call(
        functools.partial(_swiglu_ln_kernel, ln_row=ln_row),
        grid=(t // tm, d_ff // tf),
        in_specs=[
            pl.BlockSpec((tm, d), lambda i, f: (i, 0)),
            pl.BlockSpec((None, d, tf), lambda i, f: (layer, 0, f)),
            pl.BlockSpec((None, d, tf), lambda i, f: (layer, 0, f)),
            pl.BlockSpec((None, tf, d), lambda i, f: (layer, f, 0)),
            full(gains), full(biases),
        ],
        out_specs=pl.BlockSpec((tm, d), lambda i, f: (i, 0)),
        out_shape=jax.ShapeDtypeStruct((t, d), F32),
        scratch_shapes=[pltpu.VMEM((tm, d), BF16)],
        compiler_params=_params("parallel", "arbitrary"),
        name="swiglu_ln",
    )(x, w_gate, w_up, w_down, gains, biases)


def _proj_kernel(x_ref, w_ref, cos_ref, sin_ref, o_ref, xb_ref, *, rope_cols, k_cols_from, head_dim):
    j = pl.program_id(1)
    tn = o_ref.shape[1]

    @pl.when(j == 0)
    def _():
        xb_ref[...] = x_ref[...].astype(BF16)

    y = jnp.dot(xb_ref[...], w_ref[...].astype(BF16), preferred_element_type=F32)

    @pl.when(j * tn < rope_cols)
    def _():
        half = head_dim // 2
        cos, sin = cos_ref[...], sin_ref[...]
        scale = jnp.where(j * tn >= k_cols_from, head_dim ** -0.5, 1.0).astype(F32)
        parts = []
        for h in range(tn // head_dim):
            a1 = y[:, h * head_dim:h * head_dim + half]
            a2 = y[:, h * head_dim + half:(h + 1) * head_dim]
            parts.append((a1 * cos - a2 * sin) * scale)
            parts.append((a1 * sin + a2 * cos) * scale)
        o_ref[...] = jnp.concatenate(parts, axis=-1).astype(o_ref.dtype)

    @pl.when(j * tn >= rope_cols)
    def _():
        o_ref[...] = y.astype(o_ref.dtype)


def _project(x, w_in, cos, sin, *, layer, col_start, n_cols, rope_cols, head_dim, out_dtype):
    t, d = x.shape
    seq = cos.shape[0]
    tm, tn = min(PROJ_ROWS, seq), PROJ_COLS
    assert t % tm == 0 and seq % tm == 0 and n_cols % tn == 0 and col_start % tn == 0
    assert tn % head_dim == 0 and rope_cols % tn == 0 and cos.shape[1] == head_dim // 2
    col0 = col_start // tn
    seq_tiles = seq // tm
    return pl.pallas_call(
        functools.partial(_proj_kernel, rope_cols=rope_cols, k_cols_from=rope_cols // 2, head_dim=head_dim),
        grid=(t // tm, n_cols // tn),
        in_specs=[
            pl.BlockSpec((tm, d), lambda i, j: (i, 0)),
            pl.BlockSpec((None, d, tn), lambda i, j: (layer, 0, col0 + j)),
            pl.BlockSpec((tm, head_dim // 2), lambda i, j: (i % seq_tiles, 0)),
            pl.BlockSpec((tm, head_dim // 2), lambda i, j: (i % seq_tiles, 0)),
        ],
        out_specs=pl.BlockSpec((tm, tn), lambda i, j: (i, j)),
        out_shape=jax.ShapeDtypeStruct((t, n_cols), out_dtype),
        scratch_shapes=[pltpu.VMEM((tm, d), BF16)],
        compiler_params=_params("parallel", "arbitrary"),
        name="ret_proj",
    )(x, w_in, cos, sin)


def _retention_kernel(q_ref, k_ref, v_ref, g_ref, o_ref, state_ref):
    blk = pl.program_id(1)
    length = q_ref.shape[1]
    dk = q_ref.shape[2] // RET_HEADS
    dv = v_ref.shape[2] // RET_HEADS

    @pl.when(blk == 0)
    def _():
        state_ref[...] = jnp.zeros_like(state_ref)

    row = lax.broadcasted_iota(jnp.int32, (length, length), 0)
    col = lax.broadcasted_iota(jnp.int32, (length, length), 1)
    dist = jnp.abs(row - col).astype(F32)
    visible = (col // CHUNK) <= (row // CHUNK)
    pos = lax.broadcasted_iota(jnp.int32, (length, dk), 0).astype(F32)
    nt = (((1,), (1,)), ((), ()))
    tn = (((0,), (0,)), ((), ()))
    for h in range(RET_HEADS):
        lg = LOG_GAMMA[h]
        q = q_ref[0, :, h * dk:(h + 1) * dk]
        k = k_ref[0, :, h * dk:(h + 1) * dk]
        v = v_ref[0, :, h * dv:(h + 1) * dv]
        decay = jnp.where(visible, jnp.exp(dist * lg), 0.0)
        scores = lax.dot_general(q, k, nt, preferred_element_type=F32) * decay
        q_in = (q.astype(F32) * jnp.exp((pos + 1.0) * lg)).astype(BF16)
        k_out = (k.astype(F32) * jnp.exp((length - 1.0 - pos) * lg)).astype(BF16)
        state = state_ref[h]
        out = (jnp.dot(scores.astype(BF16), v, preferred_element_type=F32)
               + jnp.dot(q_in, state.astype(BF16), preferred_element_type=F32))
        state_ref[h] = math.exp(length * lg) * state + lax.dot_general(k_out, v, tn, preferred_element_type=F32)
        mu = jnp.mean(out, axis=-1, keepdims=True)
        oc = out - mu
        var = jnp.mean(oc * oc, axis=-1, keepdims=True)
        normed = oc * lax.rsqrt(var + GN_EPS)
        gate = g_ref[0, :, h * dv:(h + 1) * dv]
        o_ref[0, :, h * dv:(h + 1) * dv] = (jax.nn.silu(gate) * normed).astype(o_ref.dtype)


def _retention_core(qkv, gate, *, batch):
    t, v_dim = gate.shape
    qk_dim = (qkv.shape[1] - v_dim) // 2
    seq = t // batch
    length = min(RET_BLOCK, seq)
    assert seq % length == 0 and length % CHUNK == 0 and v_dim % qk_dim == 0
    qkv3 = qkv.reshape(batch, seq, qkv.shape[1])
    gate3 = gate.reshape(batch, seq, v_dim)
    v_block = 2 * qk_dim // v_dim
    assert v_block * v_dim == 2 * qk_dim
    out = pl.pallas_call(
        _retention_kernel,
        grid=(batch, seq // length),
        in_specs=[
            pl.BlockSpec((1, length, qk_dim), lambda b, s: (b, s, 0)),
            pl.BlockSpec((1, length, qk_dim), lambda b, s: (b, s, 1)),
            pl.BlockSpec((1, length, v_dim), lambda b, s: (b, s, v_block)),
            pl.BlockSpec((1, length, v_dim), lambda b, s: (b, s, 0)),
        ],
        out_specs=pl.BlockSpec((1, length, v_dim), lambda b, s: (b, s, 0)),
        out_shape=jax.ShapeDtypeStruct((batch, seq, v_dim), BF16),
        scratch_shapes=[pltpu.VMEM((RET_HEADS, qk_dim // RET_HEADS, v_dim // RET_HEADS), F32)],
        compiler_params=_params("parallel", "arbitrary"),
        name="ret_core",
    )(qkv3, qkv3, qkv3, gate3)
    return out.reshape(t, v_dim)


def _outproj_ln_router_kernel(o_ref, wo_ref, x_ref, gain_ref, bias_ref, wr_ref,
                              x1_ref, idx_ref, gates_ref, *, ln_row):
    kk = pl.program_id(1)
    @pl.when(kk == 0)
    def _():
        x1_ref[...] = jnp.zeros_like(x1_ref)

    x1_ref[...] += jnp.dot(o_ref[...], wo_ref[...].astype(BF16), preferred_element_type=F32)

    @pl.when(kk == pl.num_programs(1) - 1)
    def _():
        x1 = _layer_norm(ALPHA * x_ref[...] + x1_ref[...], _row(gain_ref, ln_row), _row(bias_ref, ln_row))
        x1_ref[...] = x1
        x_hi = x1.astype(BF16)
        x_lo = (x1 - x_hi.astype(F32)).astype(BF16)
        w = wr_ref[...]
        w_hi = w.astype(BF16)
        w_lo = (w - w_hi.astype(F32)).astype(BF16)
        logits = (jnp.dot(x_hi, w_hi, preferred_element_type=F32)
                  + (jnp.dot(x_hi, w_lo, preferred_element_type=F32)
                     + jnp.dot(x_lo, w_hi, preferred_element_type=F32)))
        expert = lax.broadcasted_iota(jnp.int32, logits.shape, 1).astype(F32)
        none = float(N_EXPERTS)
        m1 = jnp.max(logits, axis=1, keepdims=True)
        i1 = jnp.min(jnp.where(logits == m1, expert, none), axis=1, keepdims=True)
        rest = jnp.where(expert == i1, -jnp.inf, logits)
        m2 = jnp.max(rest, axis=1, keepdims=True)
        i2 = jnp.min(jnp.where(rest == m2, expert, none), axis=1, keepdims=True)
        p2 = jnp.exp(m2 - m1)
        denom = 1.0 + p2
        first = lax.broadcasted_iota(jnp.int32, idx_ref.shape, 1) == 0
        idx_ref[...] = jnp.where(first, i1, i2).astype(jnp.int32)
        gates_ref[...] = jnp.where(first, 1.0 / denom, p2 / denom)


def _outproj_ln_router(o, w_o, x, gains, biases, w_router, *, layer, ln_row):
    t, d = x.shape
    v_dim = o.shape[1]
    tm, tk = min(OUTPROJ_ROWS, t), min(OUTPROJ_K, v_dim)
    assert t % tm == 0 and v_dim % tk == 0 and TOP_K == 2
    full = lambda a: pl.BlockSpec(a.shape, lambda i, k: (0,) * a.ndim)
    return pl.pallas_call(
        functools.partial(_outproj_ln_router_kernel, ln_row=ln_row),
        grid=(t // tm, v_dim // tk),
        in_specs=[
            pl.BlockSpec((tm, tk), lambda i, k: (i, k)),
            pl.BlockSpec((None, tk, d), lambda i, k: (layer, k, 0)),
            pl.BlockSpec((tm, d), lambda i, k: (i, 0)),
            full(gains), full(biases),
            pl.BlockSpec((None, d, N_EXPERTS), lambda i, k: (layer, 0, 0)),
        ],
        out_specs=[
            pl.BlockSpec((tm, d), lambda i, k: (i, 0)),
            pl.BlockSpec((tm, TOP_K), lambda i, k: (i, 0)),
            pl.BlockSpec((tm, TOP_K), lambda i, k: (i, 0)),
        ],
        out_shape=[
            jax.ShapeDtypeStruct((t, d), F32),
            jax.ShapeDtypeStruct((t, TOP_K), jnp.int32),
            jax.ShapeDtypeStruct((t, TOP_K), F32),
        ],
        compiler_params=_params("parallel", "arbitrary"),
        name="outproj_ln_router",
    )(o, w_o, x, gains, biases, w_router)


def _route_plan(idx, block_rows):
    flat_e = idx.reshape(-1)
    n_assign = flat_e.shape[0]
    experts = jnp.arange(N_EXPERTS, dtype=jnp.int32)
    onehot = (flat_e[:, None] == experts[None, :]).astype(jnp.int32)
    running = jnp.cumsum(onehot, axis=0)
    rank = jnp.sum((running - onehot) * onehot, axis=1)
    counts = running[-1]
    padded = (counts + block_rows - 1) // block_rows * block_rows
    pad_end = jnp.cumsum(padded)
    pad_start = pad_end - padded
    dest = (jnp.sum(onehot * pad_start[None, :], axis=1) + rank).astype(jnp.int32)
    n_blocks = n_assign // block_rows + N_EXPERTS
    block_start = jnp.arange(n_blocks, dtype=jnp.int32) * block_rows
    block_expert = jnp.minimum(jnp.sum(block_start[:, None] >= pad_end[None, :], axis=1), N_EXPERTS - 1)
    used_blocks = (pad_end[-1] // block_rows).reshape(1)
    return dest, block_expert.astype(jnp.int32), used_blocks.astype(jnp.int32), n_blocks


def _for_each_row(n_rows, body):
    assert n_rows % DMA_UNROLL == 0

    def group(g, carry):
        for u in range(DMA_UNROLL):
            body(g * DMA_UNROLL + u)
        return carry

    lax.fori_loop(0, n_rows // DMA_UNROLL, group, 0)


def _dispatch_kernel(dest_ref, x_ref, init_hbm, xs_hbm, sem):
    del init_hbm
    tokens = x_ref.shape[0]
    base = pl.program_id(0) * tokens

    def row_copy(j, k):
        return pltpu.make_async_copy(x_ref.at[pl.ds(j, 1)], xs_hbm.at[pl.ds(dest_ref[TOP_K * (base + j) + k], 1)], sem)

    def issue(j):
        for k in range(TOP_K):
            row_copy(j, k).start()

    def drain(j):
        for k in range(TOP_K):
            row_copy(j, k).wait()

    _for_each_row(tokens, issue)
    _for_each_row(tokens, drain)


def _dispatch(x, dest, n_rows):
    t, d = x.shape
    tokens = min(ROUTE_ROWS, t)
    assert t % tokens == 0
    return pl.pallas_call(
        _dispatch_kernel,
        grid_spec=pltpu.PrefetchScalarGridSpec(
            num_scalar_prefetch=1,
            grid=(t // tokens,),
            in_specs=[pl.BlockSpec((tokens, d), lambda i, dest: (i, 0)), pl.BlockSpec(memory_space=pl.ANY)],
            out_specs=pl.BlockSpec(memory_space=pl.ANY),
            scratch_shapes=[pltpu.SemaphoreType.DMA(())],
        ),
        out_shape=jax.ShapeDtypeStruct((n_rows, d), x.dtype),
        input_output_aliases={2: 0},
        compiler_params=pltpu.CompilerParams(dimension_semantics=("arbitrary",), has_side_effects=True),
        name="moe_dispatch",
    )(dest, x, jnp.zeros((n_rows, d), x.dtype))


def _expert_kernel(block_expert_ref, used_ref, xs_ref, wg_ref, wu_ref, wd_ref, ys_ref, xb_ref):
    del block_expert_ref
    i, f = pl.program_id(0), pl.program_id(1)
    live = i < used_ref[0]

    @pl.when(f == 0)
    def _():
        xb_ref[...] = xs_ref[...].astype(BF16)
        ys_ref[...] = jnp.zeros_like(ys_ref)

    @pl.when(live)
    def _():
        ys_ref[...] += _swiglu_partial(xb_ref[...], wg_ref, wu_ref, wd_ref)


def _experts(xs, block_expert, used_blocks, w_gate, w_up, w_down, *, layer, block_rows):
    n_rows, d = xs.shape
    d_ff = w_gate.shape[3]
    tf = min(MOE_COLS, d_ff)
    assert n_rows % block_rows == 0 and d_ff % tf == 0
    n_f = d_ff // tf

    def col(i, f, used):
        return jnp.where(i < used[0], f, n_f - 1)

    return pl.pallas_call(
        _expert_kernel,
        grid_spec=pltpu.PrefetchScalarGridSpec(
            num_scalar_prefetch=2,
            grid=(n_rows // block_rows, n_f),
            in_specs=[
                pl.BlockSpec((block_rows, d), lambda i, f, be, used: (i, 0)),
                pl.BlockSpec((None, None, d, tf), lambda i, f, be, used: (layer, be[i], 0, col(i, f, used))),
                pl.BlockSpec((None, None, d, tf), lambda i, f, be, used: (layer, be[i], 0, col(i, f, used))),
                pl.BlockSpec((None, None, tf, d), lambda i, f, be, used: (layer, be[i], col(i, f, used), 0)),
            ],
            out_specs=pl.BlockSpec((block_rows, d), lambda i, f, be, used: (i, 0)),
            scratch_shapes=[pltpu.VMEM((block_rows, d), BF16)],
        ),
        out_shape=jax.ShapeDtypeStruct((n_rows, d), F32),
        compiler_params=_params("parallel", "arbitrary"),
        name="moe_experts",
    )(block_expert, used_blocks, xs, w_gate, w_up, w_down)


def _combine_ln_kernel(dest_ref, x_ref, gates_ref, gain_ref, bias_ref, ys_hbm, o_ref, y_buf, sems, *, ln_row):
    tokens = x_ref.shape[0]
    tile = pl.program_id(0)

    def row_copy(which, j, k):
        slot = which % 2
        return pltpu.make_async_copy(ys_hbm.at[pl.ds(dest_ref[TOP_K * (which * tokens + j) + k], 1)],
                                     y_buf.at[slot, k, pl.ds(j, 1)], sems.at[slot])

    def gather(which):
        def issue(j):
            for k in range(TOP_K):
                row_copy(which, j, k).start()
        _for_each_row(tokens, issue)

    def drain(j):
        for k in range(TOP_K):
            row_copy(tile, j, k).wait()

    @pl.when(tile == 0)
    def _():
        gather(tile)

    @pl.when(tile + 1 < pl.num_programs(0))
    def _():
        gather(tile + 1)

    _for_each_row(tokens, drain)
    slot = tile % 2
    gates = gates_ref[...]
    mix = gates[:, 0:1] * y_buf[slot, 0] + gates[:, 1:2] * y_buf[slot, 1]
    o_ref[...] = _layer_norm(ALPHA * x_ref[...] + mix, _row(gain_ref, ln_row), _row(bias_ref, ln_row))


def _combine_ln(x, ys, dest, gates, gains, biases, *, ln_row):
    t, d = x.shape
    tokens = min(ROUTE_ROWS, t)
    assert t % tokens == 0
    full = lambda a: pl.BlockSpec(a.shape, lambda i, dest: (0,) * a.ndim)
    return pl.pallas_call(
        functools.partial(_combine_ln_kernel, ln_row=ln_row),
        grid_spec=pltpu.PrefetchScalarGridSpec(
            num_scalar_prefetch=1,
            grid=(t // tokens,),
            in_specs=[
                pl.BlockSpec((tokens, d), lambda i, dest: (i, 0)),
                pl.BlockSpec((tokens, TOP_K), lambda i, dest: (i, 0)),
                full(gains), full(biases),
                pl.BlockSpec(memory_space=pl.ANY),
            ],
            out_specs=pl.BlockSpec((tokens, d), lambda i, dest: (i, 0)),
            scratch_shapes=[pltpu.VMEM((2, TOP_K, tokens, d), F32), pltpu.SemaphoreType.DMA((2,))],
        ),
        out_shape=jax.ShapeDtypeStruct((t, d), F32),
        compiler_params=_params("arbitrary"),
        name="moe_combine_ln",
    )(dest, x, gates, gains, biases, ys)


def _moe_ln(x, idx, gates, w_gate, w_up, w_down, gains, biases, *, layer, ln_row):
    dest, block_expert, used_blocks, n_blocks = _route_plan(idx, MOE_ROWS)
    xs = _dispatch(x, dest, n_blocks * MOE_ROWS)
    ys = _experts(xs, block_expert, used_blocks, w_gate, w_up, w_down, layer=layer, block_rows=MOE_ROWS)
    return _combine_ln(x, ys, dest, gates, gains, biases, ln_row=ln_row)


def _rope_tables(seq, head_dim):
    half = head_dim // 2
    inv_freq = ROPE_BASE ** (-jnp.arange(half, dtype=F32) / half)
    ang = jnp.arange(seq, dtype=F32)[:, None] * inv_freq[None, :]
    return jnp.cos(ang), jnp.sin(ang)


def kernel(x, ln_gain, ln_bias, pool_w, pool_scale, ret_w_in, ret_w_o, ffn_w_gate, ffn_w_up, ffn_w_down,
           moe_w_router, moe_w_gate, moe_w_up, moe_w_down):
    batch, seq, d = x.shape
    t = batch * seq
    gains = ln_gain.reshape(2 * DEPTH, d)
    biases = ln_bias.reshape(2 * DEPTH, d)
    qk_dim = ret_w_o.shape[2]
    v_dim = ret_w_o.shape[1]
    head_dim = qk_dim // RET_HEADS
    cos, sin = _rope_tables(seq, head_dim)
    pool_w = pool_w.astype(BF16)
    ret_w_in = ret_w_in.astype(BF16)
    ret_w_o = ret_w_o.astype(BF16)
    ffn_w_gate, ffn_w_up, ffn_w_down = (w.astype(BF16) for w in (ffn_w_gate, ffn_w_up, ffn_w_down))

    for i in range(DEPTH):
        j = i // 2
        if i % 2 == 0:
            x = _pool_ln(x, pool_w, pool_scale, gains, biases, layer=j, ln_row=2 * i)
            x2 = _swiglu_ln(x.reshape(t, d), ffn_w_gate, ffn_w_up, ffn_w_down, gains, biases,
                            layer=j, ln_row=2 * i + 1)
        else:
            x2 = x.reshape(t, d)
            qkv = _project(x2, ret_w_in, cos, sin, layer=j, col_start=0, n_cols=2 * qk_dim + v_dim,
                           rope_cols=2 * qk_dim, head_dim=head_dim, out_dtype=BF16)
            gate = _project(x2, ret_w_in, cos, sin, layer=j, col_start=2 * qk_dim + v_dim, n_cols=v_dim,
                            rope_cols=0, head_dim=head_dim, out_dtype=F32)
            o = _retention_core(qkv, gate, batch=batch)
            x1, idx, gates = _outproj_ln_router(o, ret_w_o, x2, gains, biases, moe_w_router,
                                                layer=j, ln_row=2 * i)
            x2 = _moe_ln(x1, idx, gates, moe_w_gate, moe_w_up, moe_w_down, gains, biases,
                         layer=j, ln_row=2 * i + 1)
        x = x2.reshape(batch, seq, d)
    return x
```

```python
import functools
import math

import jax
import jax.numpy as jnp
from jax import lax
from jax.experimental import pallas as pl
from jax.experimental.pallas import tpu as pltpu

F32 = jnp.float32
BF16 = jnp.bfloat16

DEPTH = 4
POOL_WINDOWS = (2, 4, 8, 16)
RET_HEADS = 8
CHUNK = 64
ROPE_BASE = 10000.0
N_EXPERTS = 8
TOP_K = 2
LN_EPS = 1e-5
GN_EPS = 1e-6
ALPHA = (2 * DEPTH) ** 0.25
LOG_GAMMA = tuple(math.log1p(-(2.0 ** (-5 - h))) for h in range(RET_HEADS))

V7X_VMEM_BYTES = 64 * 1024 * 1024
V7X_F32_SUBLANES = 8
POOL_HALO = 2 * V7X_F32_SUBLANES

VMEM_LIMIT = V7X_VMEM_BYTES - 8 * 1024 * 1024
POOL_ROWS = 512
PROJ_ROWS, PROJ_COLS = 1024, 1024
RET_BLOCK = 256
OUTPROJ_ROWS, OUTPROJ_K = 512, 2048
FFN_ROWS, FFN_COLS = 1024, 256
MOE_ROWS, MOE_COLS = 1024, 256
SUB_ROWS = 256
ROUTE_ROWS = 512
DMA_UNROLL = 8


def _params(*semantics):
    return pltpu.CompilerParams(dimension_semantics=semantics, vmem_limit_bytes=VMEM_LIMIT)


def _layer_norm(z, gain, bias):
    mu = jnp.mean(z, axis=-1, keepdims=True)
    zc = z - mu
    var = jnp.mean(zc * zc, axis=-1, keepdims=True)
    return zc * lax.rsqrt(var + LN_EPS) * gain + bias


def _row(ref, r):
    return ref[r:r + 1, :]


def _pool_ln_kernel(x_ref, halo_ref, w_ref, scale_ref, gain_ref, bias_ref, o_ref, *, layer, ln_row):
    ts, d = x_ref.shape[1], x_ref.shape[2]
    group = d // len(POOL_WINDOWS)
    s = pl.program_id(1)
    x = x_ref[0]
    halo = jnp.where(s > 0, halo_ref[0], 0.0)
    ext = jnp.concatenate([halo, x], axis=0)
    seen = (s * ts + 1 + lax.broadcasted_iota(jnp.int32, (ts, 1), 0)).astype(F32)
    mixed = []
    for g, win in enumerate(POOL_WINDOWS):
        cols = slice(g * group, (g + 1) * group)
        acc = ext[:, cols]
        span = 1
        while span < win:
            acc = acc + pltpu.roll(acc, span, 0)
            span *= 2
        pooled = acc[POOL_HALO:] * (1.0 / jnp.minimum(seen, float(win)))
        diff = (pooled - x[:, cols]).astype(BF16)
        mixed.append(jnp.dot(diff, w_ref[g].astype(BF16), preferred_element_type=F32))
    mix = jnp.concatenate(mixed, axis=-1) * _row(scale_ref, layer)
    o_ref[0] = _layer_norm(ALPHA * x + mix, _row(gain_ref, ln_row), _row(bias_ref, ln_row))


def _pool_ln(x, pool_w, pool_scale, gains, biases, *, layer, ln_row):
    b, s, d = x.shape
    ts = min(POOL_ROWS, s)
    assert s % ts == 0 and ts % POOL_HALO == 0 and max(POOL_WINDOWS) <= POOL_HALO
    halo_blocks = ts // POOL_HALO
    n_groups, group = pool_w.shape[1], pool_w.shape[2]
    full = lambda a: pl.BlockSpec(a.shape, lambda bi, si: (0,) * a.ndim)
    return pl.pallas_call(
        functools.partial(_pool_ln_kernel, layer=layer, ln_row=ln_row),
        grid=(b, s // ts),
        in_specs=[
            pl.BlockSpec((1, ts, d), lambda bi, si: (bi, si, 0)),
            pl.BlockSpec((1, POOL_HALO, d), lambda bi, si: (bi, jnp.maximum(si * halo_blocks - 1, 0), 0)),
            pl.BlockSpec((None, n_groups, group, group), lambda bi, si: (layer, 0, 0, 0)),
            full(pool_scale), full(gains), full(biases),
        ],
        out_specs=pl.BlockSpec((1, ts, d), lambda bi, si: (bi, si, 0)),
        out_shape=jax.ShapeDtypeStruct(x.shape, F32),
        compiler_params=_params("parallel", "parallel"),
        name="pool_ln",
    )(x, x, pool_w, pool_scale, gains, biases)


def _cast_weights(w_refs, wb_refs):
    for w_ref, wb_ref in zip(w_refs, wb_refs):
        wb_ref[...] = w_ref[...].astype(BF16)


def _swiglu_rows(rows, xb_ref, wgb_ref, wub_ref, wdb_ref, acc_ref):
    xb = xb_ref[rows, :]
    gate = jnp.dot(xb, wgb_ref[...], preferred_element_type=F32)
    up = jnp.dot(xb, wub_ref[...], preferred_element_type=F32)
    hidden = (jax.nn.silu(gate) * up).astype(BF16)
    acc_ref[rows, :] += jnp.dot(hidden, wdb_ref[...], preferred_element_type=F32)


def _swiglu_ln_kernel(x_ref, wg_ref, wu_ref, wd_ref, gain_ref, bias_ref, o_ref,
                      xb_ref, wgb_ref, wub_ref, wdb_ref, *, ln_row):
    f = pl.program_id(1)
    sub_blocks = [pl.ds(r, SUB_ROWS) for r in range(0, x_ref.shape[0], SUB_ROWS)]

    @pl.when(f == 0)
    def _():
        xb_ref[...] = x_ref[...].astype(BF16)
        o_ref[...] = jnp.zeros_like(o_ref)

    _cast_weights((wg_ref, wu_ref, wd_ref), (wgb_ref, wub_ref, wdb_ref))
    for rows in sub_blocks:
        _swiglu_rows(rows, xb_ref, wgb_ref, wub_ref, wdb_ref, o_ref)

    @pl.when(f == pl.num_programs(1) - 1)
    def _():
        for rows in sub_blocks:
            o_ref[rows, :] = _layer_norm(ALPHA * x_ref[rows, :] + o_ref[rows, :],
                                         _row(gain_ref, ln_row), _row(bias_ref, ln_row))


def _swiglu_ln(x, w_gate, w_up, w_down, gains, biases, *, layer, ln_row):
    t, d = x.shape
    d_ff = w_gate.shape[2]
    tm, tf = min(FFN_ROWS, t), min(FFN_COLS, d_ff)
    assert t % tm == 0 and d_ff % tf == 0 and tm % SUB_ROWS == 0
    full = lambda a: pl.BlockSpec(a.shape, lambda i, f: (0,) * a.ndim)
    return pl.pallas_call(
        functools.partial(_swiglu_ln_kernel, ln_row=ln_row),
        grid=(t // tm, d_ff // tf),
        in_specs=[
            pl.BlockSpec((tm, d), lambda i, f: (i, 0)),
            pl.BlockSpec((None, d, tf), lambda i, f: (layer, 0, f)),
            pl.BlockSpec((None, d, tf), lambda i, f: (layer, 0, f)),
            pl.BlockSpec((None, tf, d), lambda i, f: (layer, f, 0)),
            full(gains), full(biases),
        ],
        out_specs=pl.BlockSpec((tm, d), lambda i, f: (i, 0)),
        out_shape=jax.ShapeDtypeStruct((t, d), F32),
        scratch_shapes=[pltpu.VMEM((tm, d), BF16), pltpu.VMEM((d, tf), BF16), pltpu.VMEM((d, tf), BF16),
                        pltpu.VMEM((tf, d), BF16)],
        compiler_params=_params("parallel", "arbitrary"),
        name="swiglu_ln",
    )(x, w_gate, w_up, w_down, gains, biases)


def _proj_kernel(x_ref, w_ref, cos_ref, sin_ref, o_ref, xb_ref, *, rope_cols, k_cols_from, head_dim):
    j = pl.program_id(1)
    tn = o_ref.shape[1]

    @pl.when(j == 0)
    def _():
        xb_ref[...] = x_ref[...].astype(BF16)

    y = jnp.dot(xb_ref[...], w_ref[...].astype(BF16), preferred_element_type=F32)

    @pl.when(j * tn < rope_cols)
    def _():
        half = head_dim // 2
        cos, sin = cos_ref[...], sin_ref[...]
        scale = jnp.where(j * tn >= k_cols_from, head_dim ** -0.5, 1.0).astype(F32)
        parts = []
        for h in range(tn // head_dim):
            a1 = y[:, h * head_dim:h * head_dim + half]
            a2 = y[:, h * head_dim + half:(h + 1) * head_dim]
            parts.append((a1 * cos - a2 * sin) * scale)
            parts.append((a1 * sin + a2 * cos) * scale)
        o_ref[...] = jnp.concatenate(parts, axis=-1).astype(o_ref.dtype)

    @pl.when(j * tn >= rope_cols)
    def _():
        o_ref[...] = y.astype(o_ref.dtype)


def _project(x, w_in, cos, sin, *, layer, col_start, n_cols, rope_cols, head_dim, out_dtype):
    t, d = x.shape
    seq = cos.shape[0]
    tm, tn = min(PROJ_ROWS, seq), PROJ_COLS
    assert t % tm == 0 and seq % tm == 0 and n_cols % tn == 0 and col_start % tn == 0
    assert tn % head_dim == 0 and rope_cols % tn == 0 and cos.shape[1] == head_dim // 2
    col0 = col_start // tn
    seq_tiles = seq // tm
    return pl.pallas_call(
        functools.partial(_proj_kernel, rope_cols=rope_cols, k_cols_from=rope_cols // 2, head_dim=head_dim),
        grid=(t // tm, n_cols // tn),
        in_specs=[
            pl.BlockSpec((tm, d), lambda i, j: (i, 0)),
            pl.BlockSpec((None, d, tn), lambda i, j: (layer, 0, col0 + j)),
            pl.BlockSpec((tm, head_dim // 2), lambda i, j: (i % seq_tiles, 0)),
            pl.BlockSpec((tm, head_dim // 2), lambda i, j: (i % seq_tiles, 0)),
        ],
        out_specs=pl.BlockSpec((tm, tn), lambda i, j: (i, j)),
        out_shape=jax.ShapeDtypeStruct((t, n_cols), out_dtype),
        scratch_shapes=[pltpu.VMEM((tm, d), BF16)],
        compiler_params=_params("parallel", "arbitrary"),
        name="ret_proj",
    )(x, w_in, cos, sin)


def _retention_kernel(q_ref, k_ref, v_ref, g_ref, o_ref, state_ref):
    blk = pl.program_id(1)
    length = q_ref.shape[1]
    dk = q_ref.shape[2] // RET_HEADS
    dv = v_ref.shape[2] // RET_HEADS

    @pl.when(blk == 0)
    def _():
        state_ref[...] = jnp.zeros_like(state_ref)

    row = lax.broadcasted_iota(jnp.int32, (length, length), 0)
    col = lax.broadcasted_iota(jnp.int32, (length, length), 1)
    dist = jnp.abs(row - col).astype(F32)
    visible = (col // CHUNK) <= (row // CHUNK)
    pos = lax.broadcasted_iota(jnp.int32, (length, dk), 0).astype(F32)
    nt = (((1,), (1,)), ((), ()))
    tn = (((0,), (0,)), ((), ()))
    for h in range(RET_HEADS):
        lg = LOG_GAMMA[h]
        q = q_ref[0, :, h * dk:(h + 1) * dk]
        k = k_ref[0, :, h * dk:(h + 1) * dk]
        v = v_ref[0, :, h * dv:(h + 1) * dv]
        decay = jnp.where(visible, jnp.exp(dist * lg), 0.0)
        scores = lax.dot_general(q, k, nt, preferred_element_type=F32) * decay
        q_in = (q.astype(F32) * jnp.exp((pos + 1.0) * lg)).astype(BF16)
        k_out = (k.astype(F32) * jnp.exp((length - 1.0 - pos) * lg)).astype(BF16)
        state = state_ref[h]
        out = (jnp.dot(scores.astype(BF16), v, preferred_element_type=F32)
               + jnp.dot(q_in, state.astype(BF16), preferred_element_type=F32))
        state_ref[h] = math.exp(length * lg) * state + lax.dot_general(k_out, v, tn, preferred_element_type=F32)
        mu = jnp.mean(out, axis=-1, keepdims=True)
        oc = out - mu
        var = jnp.mean(oc * oc, axis=-1, keepdims=True)
        normed = oc * lax.rsqrt(var + GN_EPS)
        gate = g_ref[0, :, h * dv:(h + 1) * dv]
        o_ref[0, :, h * dv:(h + 1) * dv] = (jax.nn.silu(gate) * normed).astype(o_ref.dtype)


def _retention_core(qkv, gate, *, batch):
    t, v_dim = gate.shape
    qk_dim = (qkv.shape[1] - v_dim) // 2
    seq = t // batch
    length = min(RET_BLOCK, seq)
    assert seq % length == 0 and length % CHUNK == 0 and v_dim % qk_dim == 0
    qkv3 = qkv.reshape(batch, seq, qkv.shape[1])
    gate3 = gate.reshape(batch, seq, v_dim)
    v_block = 2 * qk_dim // v_dim
    assert v_block * v_dim == 2 * qk_dim
    out = pl.pallas_call(
        _retention_kernel,
        grid=(batch, seq // length),
        in_specs=[
            pl.BlockSpec((1, length, qk_dim), lambda b, s: (b, s, 0)),
            pl.BlockSpec((1, length, qk_dim), lambda b, s: (b, s, 1)),
            pl.BlockSpec((1, length, v_dim), lambda b, s: (b, s, v_block)),
            pl.BlockSpec((1, length, v_dim), lambda b, s: (b, s, 0)),
        ],
        out_specs=pl.BlockSpec((1, length, v_dim), lambda b, s: (b, s, 0)),
        out_shape=jax.ShapeDtypeStruct((batch, seq, v_dim), BF16),
        scratch_shapes=[pltpu.VMEM((RET_HEADS, qk_dim // RET_HEADS, v_dim // RET_HEADS), F32)],
        compiler_params=_params("parallel", "arbitrary"),
        name="ret_core",
    )(qkv3, qkv3, qkv3, gate3)
    return out.reshape(t, v_dim)


def _pack_bf16_pairs(x):
    half = x.shape[1] // 2
    lo = lax.bitcast_convert_type(x[:, :half].astype(BF16).astype(F32), jnp.uint32)
    hi = lax.bitcast_convert_type(x[:, half:].astype(BF16).astype(F32), jnp.uint32)
    return hi | (lo >> 16)


def _unpack_bf16_pairs(xp):
    lo = lax.bitcast_convert_type(xp << 16, F32).astype(BF16)
    hi = lax.bitcast_convert_type(xp & jnp.uint32(0xFFFF0000), F32).astype(BF16)
    return jnp.concatenate([lo, hi], axis=1)


def _outproj_ln_router_kernel(o_ref, wo_ref, x_ref, gain_ref, bias_ref, wr_ref,
                              x1_ref, xp_ref, idx_ref, gates_ref, *, ln_row):
    kk = pl.program_id(1)

    @pl.when(kk == 0)
    def _():
        x1_ref[...] = jnp.zeros_like(x1_ref)

    x1_ref[...] += jnp.dot(o_ref[...], wo_ref[...].astype(BF16), preferred_element_type=F32)

    @pl.when(kk == pl.num_programs(1) - 1)
    def _():
        x1 = _layer_norm(ALPHA * x_ref[...] + x1_ref[...], _row(gain_ref, ln_row), _row(bias_ref, ln_row))
        x1_ref[...] = x1
        xp_ref[...] = _pack_bf16_pairs(x1)
        x_hi = x1.astype(BF16)
        x_lo = (x1 - x_hi.astype(F32)).astype(BF16)
        w = wr_ref[...]
        w_hi = w.astype(BF16)
        w_lo = (w - w_hi.astype(F32)).astype(BF16)
        logits = (jnp.dot(x_hi, w_hi, preferred_element_type=F32)
                  + (jnp.dot(x_hi, w_lo, preferred_element_type=F32)
                     + jnp.dot(x_lo, w_hi, preferred_element_type=F32)))
        expert = lax.broadcasted_iota(jnp.int32, logits.shape, 1).astype(F32)
        none = float(N_EXPERTS)
        m1 = jnp.max(logits, axis=1, keepdims=True)
        i1 = jnp.min(jnp.where(logits == m1, expert, none), axis=1, keepdims=True)
        rest = jnp.where(expert == i1, -jnp.inf, logits)
        m2 = jnp.max(rest, axis=1, keepdims=True)
        i2 = jnp.min(jnp.where(rest == m2, expert, none), axis=1, keepdims=True)
        p2 = jnp.exp(m2 - m1)
        denom = 1.0 + p2
        first = lax.broadcasted_iota(jnp.int32, idx_ref.shape, 1) == 0
        idx_ref[...] = jnp.where(first, i1, i2).astype(jnp.int32)
        gates_ref[...] = jnp.where(first, 1.0 / denom, p2 / denom)


def _outproj_ln_router(o, w_o, x, gains, biases, w_router, *, layer, ln_row):
    t, d = x.shape
    v_dim = o.shape[1]
    tm, tk = min(OUTPROJ_ROWS, t), min(OUTPROJ_K, v_dim)
    assert t % tm == 0 and v_dim % tk == 0 and TOP_K == 2
    full = lambda a: pl.BlockSpec(a.shape, lambda i, k: (0,) * a.ndim)
    return pl.pallas_call(
        functools.partial(_outproj_ln_router_kernel, ln_row=ln_row),
        grid=(t // tm, v_dim // tk),
        in_specs=[
            pl.BlockSpec((tm, tk), lambda i, k: (i, k)),
            pl.BlockSpec((None, tk, d), lambda i, k: (layer, k, 0)),
            pl.BlockSpec((tm, d), lambda i, k: (i, 0)),
            full(gains), full(biases),
            pl.BlockSpec((None, d, N_EXPERTS), lambda i, k: (layer, 0, 0)),
        ],
        out_specs=[
            pl.BlockSpec((tm, d), lambda i, k: (i, 0)),
            pl.BlockSpec((tm, d // 2), lambda i, k: (i, 0)),
            pl.BlockSpec((tm, TOP_K), lambda i, k: (i, 0)),
            pl.BlockSpec((tm, TOP_K), lambda i, k: (i, 0)),
        ],
        out_shape=[
            jax.ShapeDtypeStruct((t, d), F32),
            jax.ShapeDtypeStruct((t, d // 2), jnp.uint32),
            jax.ShapeDtypeStruct((t, TOP_K), jnp.int32),
            jax.ShapeDtypeStruct((t, TOP_K), F32),
        ],
        compiler_params=_params("parallel", "arbitrary"),
        name="outproj_ln_router",
    )(o, w_o, x, gains, biases, w_router)


def _route_plan(idx, block_rows):
    flat_e = idx.reshape(-1)
    n_assign = flat_e.shape[0]
    experts = jnp.arange(N_EXPERTS, dtype=jnp.int32)
    onehot = (flat_e[:, None] == experts[None, :]).astype(jnp.int32)
    running = jnp.cumsum(onehot, axis=0)
    rank = jnp.sum((running - onehot) * onehot, axis=1)
    counts = running[-1]
    padded = (counts + block_rows - 1) // block_rows * block_rows
    pad_end = jnp.cumsum(padded)
    pad_start = pad_end - padded
    dest = (jnp.sum(onehot * pad_start[None, :], axis=1) + rank).astype(jnp.int32)
    n_blocks = n_assign // block_rows + N_EXPERTS
    block_start = jnp.arange(n_blocks, dtype=jnp.int32) * block_rows
    block_expert = jnp.minimum(jnp.sum(block_start[:, None] >= pad_end[None, :], axis=1), N_EXPERTS - 1)
    live_rows = jnp.clip((pad_start + counts)[block_expert] - block_start, 0, block_rows)
    live_sub = (live_rows + SUB_ROWS - 1) // SUB_ROWS
    return dest, block_expert.astype(jnp.int32), live_sub.astype(jnp.int32), n_blocks


def _for_each_row(n_rows, body):
    assert n_rows % DMA_UNROLL == 0

    def group(g, carry):
        for u in range(DMA_UNROLL):
            body(g * DMA_UNROLL + u)
        return carry

    lax.fori_loop(0, n_rows // DMA_UNROLL, group, 0)


def _dispatch_kernel(dest_ref, x_ref, init_hbm, xs_hbm, sem):
    del init_hbm
    tokens = x_ref.shape[0]
    base = pl.program_id(0) * tokens

    def row_copy(j, k):
        return pltpu.make_async_copy(x_ref.at[pl.ds(j, 1)], xs_hbm.at[pl.ds(dest_ref[TOP_K * (base + j) + k], 1)], sem)

    def issue(j):
        for k in range(TOP_K):
            row_copy(j, k).start()

    def drain(j):
        for k in range(TOP_K):
            row_copy(j, k).wait()

    _for_each_row(tokens, issue)
    _for_each_row(tokens, drain)


def _dispatch(x, dest, n_rows):
    t, d = x.shape
    tokens = min(ROUTE_ROWS, t)
    assert t % tokens == 0
    return pl.pallas_call(
        _dispatch_kernel,
        grid_spec=pltpu.PrefetchScalarGridSpec(
            num_scalar_prefetch=1,
            grid=(t // tokens,),
            in_specs=[pl.BlockSpec((tokens, d), lambda i, dest: (i, 0)), pl.BlockSpec(memory_space=pl.ANY)],
            out_specs=pl.BlockSpec(memory_space=pl.ANY),
            scratch_shapes=[pltpu.SemaphoreType.DMA(())],
        ),
        out_shape=jax.ShapeDtypeStruct((n_rows, d), x.dtype),
        input_output_aliases={2: 0},
        compiler_params=pltpu.CompilerParams(dimension_semantics=("arbitrary",), has_side_effects=True),
        name="moe_dispatch",
    )(dest, x, jnp.zeros((n_rows, d), x.dtype))


def _expert_kernel(block_expert_ref, live_sub_ref, xs_ref, wg_ref, wu_ref, wd_ref, ys_ref,
                   xb_ref, wgb_ref, wub_ref, wdb_ref):
    del block_expert_ref
    i, f = pl.program_id(0), pl.program_id(1)
    live_sub = live_sub_ref[i]

    @pl.when(f == 0)
    def _():
        xb_ref[...] = _unpack_bf16_pairs(xs_ref[...])
        ys_ref[...] = jnp.zeros_like(ys_ref)

    n_sub = xs_ref.shape[0] // SUB_ROWS
    sub_blocks = [pl.ds(sb * SUB_ROWS, SUB_ROWS) for sb in range(n_sub)]

    @pl.when(live_sub == n_sub)
    def _():
        _cast_weights((wg_ref, wu_ref, wd_ref), (wgb_ref, wub_ref, wdb_ref))
        for rows in sub_blocks:
            _swiglu_rows(rows, xb_ref, wgb_ref, wub_ref, wdb_ref, ys_ref)

    @pl.when(jnp.logical_and(live_sub > 0, live_sub < n_sub))
    def _():
        _cast_weights((wg_ref, wu_ref, wd_ref), (wgb_ref, wub_ref, wdb_ref))
        for sb in range(n_sub - 1):
            @pl.when(sb < live_sub)
            def _():
                _swiglu_rows(sub_blocks[sb], xb_ref, wgb_ref, wub_ref, wdb_ref, ys_ref)


def _experts(xs, block_expert, live_sub, w_gate, w_up, w_down, *, layer, block_rows):
    n_rows = xs.shape[0]
    d, d_ff = w_gate.shape[2], w_gate.shape[3]
    tf = min(MOE_COLS, d_ff)
    assert n_rows % block_rows == 0 and d_ff % tf == 0 and block_rows % SUB_ROWS == 0 and xs.shape[1] * 2 == d
    n_f = d_ff // tf

    def col(i, f, live):
        return jnp.where(live[i] > 0, f, n_f - 1)

    return pl.pallas_call(
        _expert_kernel,
        grid_spec=pltpu.PrefetchScalarGridSpec(
            num_scalar_prefetch=2,
            grid=(n_rows // block_rows, n_f),
            in_specs=[
                pl.BlockSpec((block_rows, d // 2), lambda i, f, be, live: (i, 0)),
                pl.BlockSpec((None, None, d, tf), lambda i, f, be, live: (layer, be[i], 0, col(i, f, live))),
                pl.BlockSpec((None, None, d, tf), lambda i, f, be, live: (layer, be[i], 0, col(i, f, live))),
                pl.BlockSpec((None, None, tf, d), lambda i, f, be, live: (layer, be[i], col(i, f, live), 0)),
            ],
            out_specs=pl.BlockSpec((block_rows, d), lambda i, f, be, live: (i, 0)),
            scratch_shapes=[pltpu.VMEM((block_rows, d), BF16), pltpu.VMEM((d, tf), BF16),
                            pltpu.VMEM((d, tf), BF16), pltpu.VMEM((tf, d), BF16)],
        ),
        out_shape=jax.ShapeDtypeStruct((n_rows, d), F32),
        compiler_params=_params("parallel", "arbitrary"),
        name="moe_experts",
    )(block_expert, live_sub, xs, w_gate, w_up, w_down)


def _combine_ln_kernel(dest_ref, x_ref, gates_ref, gain_ref, bias_ref, ys_hbm, o_ref, y_buf, sems, *, ln_row):
    tokens = x_ref.shape[0]
    tile = pl.program_id(0)

    def row_copy(which, j, k):
        slot = which % 2
        return pltpu.make_async_copy(ys_hbm.at[pl.ds(dest_ref[TOP_K * (which * tokens + j) + k], 1)],
                                     y_buf.at[slot, k, pl.ds(j, 1)], sems.at[slot])

    def gather(which):
        def issue(j):
            for k in range(TOP_K):
                row_copy(which, j, k).start()
        _for_each_row(tokens, issue)

    def drain(j):
        for k in range(TOP_K):
            row_copy(tile, j, k).wait()

    @pl.when(tile == 0)
    def _():
        gather(tile)

    @pl.when(tile + 1 < pl.num_programs(0))
    def _():
        gather(tile + 1)

    _for_each_row(tokens, drain)
    slot = tile % 2
    gates = gates_ref[...]
    mix = gates[:, 0:1] * y_buf[slot, 0] + gates[:, 1:2] * y_buf[slot, 1]
    o_ref[...] = _layer_norm(ALPHA * x_ref[...] + mix, _row(gain_ref, ln_row), _row(bias_ref, ln_row))


def _combine_ln(x, ys, dest, gates, gains, biases, *, ln_row):
    t, d = x.shape
    tokens = min(ROUTE_ROWS, t)
    assert t % tokens == 0
    full = lambda a: pl.BlockSpec(a.shape, lambda i, dest: (0,) * a.ndim)
    return pl.pallas_call(
        functools.partial(_combine_ln_kernel, ln_row=ln_row),
        grid_spec=pltpu.PrefetchScalarGridSpec(
            num_scalar_prefetch=1,
            grid=(t // tokens,),
            in_specs=[
                pl.BlockSpec((tokens, d), lambda i, dest: (i, 0)),
                pl.BlockSpec((tokens, TOP_K), lambda i, dest: (i, 0)),
                full(gains), full(biases),
                pl.BlockSpec(memory_space=pl.ANY),
            ],
            out_specs=pl.BlockSpec((tokens, d), lambda i, dest: (i, 0)),
            scratch_shapes=[pltpu.VMEM((2, TOP_K, tokens, d), F32), pltpu.SemaphoreType.DMA((2,))],
        ),
        out_shape=jax.ShapeDtypeStruct((t, d), F32),
        compiler_params=_params("arbitrary"),
        name="moe_combine_ln",
    )(dest, x, gates, gains, biases, ys)


def _moe_ln(x, x_packed, idx, gates, w_gate, w_up, w_down, gains, biases, *, layer, ln_row):
    dest, block_expert, live_sub, n_blocks = _route_plan(idx, MOE_ROWS)
    xs = _dispatch(x_packed, dest, n_blocks * MOE_ROWS)
    ys = _experts(xs, block_expert, live_sub, w_gate, w_up, w_down, layer=layer, block_rows=MOE_ROWS)
    return _combine_ln(x, ys, dest, gates, gains, biases, ln_row=ln_row)


def _rope_tables(seq, head_dim):
    half = head_dim // 2
    inv_freq = ROPE_BASE ** (-jnp.arange(half, dtype=F32) / half)
    ang = jnp.arange(seq, dtype=F32)[:, None] * inv_freq[None, :]
    return jnp.cos(ang), jnp.sin(ang)


def kernel(x, ln_gain, ln_bias, pool_w, pool_scale, ret_w_in, ret_w_o, ffn_w_gate, ffn_w_up, ffn_w_down,
           moe_w_router, moe_w_gate, moe_w_up, moe_w_down):
    batch, seq, d = x.shape
    t = batch * seq
    gains = ln_gain.reshape(2 * DEPTH, d)
    biases = ln_bias.reshape(2 * DEPTH, d)
    qk_dim = ret_w_o.shape[2]
    v_dim = ret_w_o.shape[1]
    head_dim = qk_dim // RET_HEADS
    cos, sin = _rope_tables(seq, head_dim)
    pool_w = pool_w.astype(BF16)
    ret_w_in = ret_w_in.astype(BF16)
    ret_w_o = ret_w_o.astype(BF16)

    for i in range(DEPTH):
        j = i // 2
        if i % 2 == 0:
            x = _pool_ln(x, pool_w, pool_scale, gains, biases, layer=j, ln_row=2 * i)
            x2 = _swiglu_ln(x.reshape(t, d), ffn_w_gate, ffn_w_up, ffn_w_down, gains, biases,
                            layer=j, ln_row=2 * i + 1)
        else:
            x2 = x.reshape(t, d)
            qkv = _project(x2, ret_w_in, cos, sin, layer=j, col_start=0, n_cols=2 * qk_dim + v_dim,
                           rope_cols=2 * qk_dim, head_dim=head_dim, out_dtype=BF16)
            gate = _project(x2, ret_w_in, cos, sin, layer=j, col_start=2 * qk_dim + v_dim, n_cols=v_dim,
                            rope_cols=0, head_dim=head_dim, out_dtype=F32)
            o = _retention_core(qkv, gate, batch=batch)
            x1, x1_packed, idx, gates = _outproj_ln_router(o, ret_w_o, x2, gains, biases, moe_w_router,
                                                           layer=j, ln_row=2 * i)
            x2 = _moe_ln(x1, x1_packed, idx, gates, moe_w_gate, moe_w_up, moe_w_down, gains, biases,
                         layer=j, ln_row=2 * i + 1)
        x = x2.reshape(batch, seq, d)
    return x
```

```python
import functools
import math

import jax
import jax.numpy as jnp
from jax import lax
from jax.experimental import pallas as pl
from jax.experimental.pallas import tpu as pltpu

F32 = jnp.float32
BF16 = jnp.bfloat16

DEPTH = 4
POOL_WINDOWS = (2, 4, 8, 16)
RET_HEADS = 8
CHUNK = 64
ROPE_BASE = 10000.0
N_EXPERTS = 8
TOP_K = 2
LN_EPS = 1e-5
GN_EPS = 1e-6
ALPHA = (2 * DEPTH) ** 0.25
LOG_GAMMA = tuple(math.log1p(-(2.0 ** (-5 - h))) for h in range(RET_HEADS))

V7X_VMEM_BYTES = 64 * 1024 * 1024
V7X_F32_SUBLANES = 8
POOL_HALO = 2 * V7X_F32_SUBLANES

VMEM_LIMIT = V7X_VMEM_BYTES - 8 * 1024 * 1024
POOL_ROWS = 512
PROJ_ROWS, PROJ_COLS = 1024, 1024
RET_BLOCK = 256
OUTPROJ_ROWS, OUTPROJ_K = 512, 2048
FFN_ROWS, FFN_COLS = 512, 512
MOE_ROWS, MOE_COLS = 1024, 256
SUB_ROWS = 256
ROUTE_ROWS = 512
DMA_UNROLL = 8


def _params(*semantics):
    return pltpu.CompilerParams(dimension_semantics=semantics, vmem_limit_bytes=VMEM_LIMIT)


def _layer_norm(z, gain, bias):
    mu = jnp.mean(z, axis=-1, keepdims=True)
    zc = z - mu
    var = jnp.mean(zc * zc, axis=-1, keepdims=True)
    return zc * lax.rsqrt(var + LN_EPS) * gain + bias


def _row(ref, r):
    return ref[r:r + 1, :]


def _pool_ln_kernel(x_ref, halo_ref, w_ref, scale_ref, gain_ref, bias_ref, o_ref, *, layer, ln_row):
    ts, d = x_ref.shape[1], x_ref.shape[2]
    group = d // len(POOL_WINDOWS)
    s = pl.program_id(1)
    x = x_ref[0]
    halo = jnp.where(s > 0, halo_ref[0], 0.0)
    ext = jnp.concatenate([halo, x], axis=0)
    seen = (s * ts + 1 + lax.broadcasted_iota(jnp.int32, (ts, 1), 0)).astype(F32)
    mixed = []
    for g, win in enumerate(POOL_WINDOWS):
        cols = slice(g * group, (g + 1) * group)
        acc = ext[:, cols]
        span = 1
        while span < win:
            acc = acc + pltpu.roll(acc, span, 0)
            span *= 2
        pooled = acc[POOL_HALO:] * (1.0 / jnp.minimum(seen, float(win)))
        diff = (pooled - x[:, cols]).astype(BF16)
        mixed.append(jnp.dot(diff, w_ref[g].astype(BF16), preferred_element_type=F32))
    mix = jnp.concatenate(mixed, axis=-1) * _row(scale_ref, layer)
    o_ref[0] = _layer_norm(ALPHA * x + mix, _row(gain_ref, ln_row), _row(bias_ref, ln_row))


def _pool_ln(x, pool_w, pool_scale, gains, biases, *, layer, ln_row):
    b, s, d = x.shape
    ts = min(POOL_ROWS, s)
    assert s % ts == 0 and ts % POOL_HALO == 0 and max(POOL_WINDOWS) <= POOL_HALO
    halo_blocks = ts // POOL_HALO
    n_groups, group = pool_w.shape[1], pool_w.shape[2]
    full = lambda a: pl.BlockSpec(a.shape, lambda bi, si: (0,) * a.ndim)
    return pl.pallas_call(
        functools.partial(_pool_ln_kernel, layer=layer, ln_row=ln_row),
        grid=(b, s // ts),
        in_specs=[
            pl.BlockSpec((1, ts, d), lambda bi, si: (bi, si, 0)),
            pl.BlockSpec((1, POOL_HALO, d), lambda bi, si: (bi, jnp.maximum(si * halo_blocks - 1, 0), 0)),
            pl.BlockSpec((None, n_groups, group, group), lambda bi, si: (layer, 0, 0, 0)),
            full(pool_scale), full(gains), full(biases),
        ],
        out_specs=pl.BlockSpec((1, ts, d), lambda bi, si: (bi, si, 0)),
        out_shape=jax.ShapeDtypeStruct(x.shape, F32),
        compiler_params=_params("parallel", "parallel"),
        name="pool_ln",
    )(x, x, pool_w, pool_scale, gains, biases)


def _cast_weights(w_refs, wb_refs):
    for w_ref, wb_ref in zip(w_refs, wb_refs):
        wb_ref[...] = w_ref[...].astype(BF16)


def _swiglu_rows(rows, xb_ref, wgb_ref, wub_ref, wdb_ref, acc_ref):
    xb = xb_ref[rows, :]
    gate = jnp.dot(xb, wgb_ref[...], preferred_element_type=F32)
    up = jnp.dot(xb, wub_ref[...], preferred_element_type=F32)
    hidden = (jax.nn.silu(gate) * up).astype(BF16)
    acc_ref[rows, :] += jnp.dot(hidden, wdb_ref[...], preferred_element_type=F32)


def _swiglu_ln_kernel(x_ref, wg_ref, wu_ref, wd_ref, gain_ref, bias_ref, o_ref, xb_ref, *, ln_row):
    f = pl.program_id(1)
    sub_blocks = [pl.ds(r, SUB_ROWS) for r in range(0, x_ref.shape[0], SUB_ROWS)]

    @pl.when(f == 0)
    def _():
        xb_ref[...] = x_ref[...].astype(BF16)
        o_ref[...] = jnp.zeros_like(o_ref)

    _swiglu_rows(pl.ds(0, x_ref.shape[0]), xb_ref, wg_ref, wu_ref, wd_ref, o_ref)

    @pl.when(f == pl.num_programs(1) - 1)
    def _():
        for rows in sub_blocks:
            o_ref[rows, :] = _layer_norm(ALPHA * x_ref[rows, :] + o_ref[rows, :],
                                         _row(gain_ref, ln_row), _row(bias_ref, ln_row))


def _swiglu_ln(x, w_gate, w_up, w_down, gains, biases, *, layer, ln_row):
    t, d = x.shape
    d_ff = w_gate.shape[2]
    tm, tf = min(FFN_ROWS, t), min(FFN_COLS, d_ff)
    assert t % tm == 0 and d_ff % tf == 0 and tm % SUB_ROWS == 0
    full = lambda a: pl.BlockSpec(a.shape, lambda i, f: (0,) * a.ndim)
    return pl.pallas_call(
        functools.partial(_swiglu_ln_kernel, ln_row=ln_row),
        grid=(t // tm, d_ff // tf),
        in_specs=[
            pl.BlockSpec((tm, d), lambda i, f: (i, 0)),
            pl.BlockSpec((None, d, tf), lambda i, f: (layer, 0, f)),
            pl.BlockSpec((None, d, tf), lambda i, f: (layer, 0, f)),
            pl.BlockSpec((None, tf, d), lambda i, f: (layer, f, 0)),
            full(gains), full(biases),
        ],
        out_specs=pl.BlockSpec((tm, d), lambda i, f: (i, 0)),
        out_shape=jax.ShapeDtypeStruct((t, d), F32),
        scratch_shapes=[pltpu.VMEM((tm, d), BF16)],
        compiler_params=_params("parallel", "arbitrary"),
        name="swiglu_ln",
    )(x, w_gate, w_up, w_down, gains, biases)


def _proj_kernel(x_ref, w_ref, cos_ref, sin_ref, o_ref, xb_ref, *, rope_cols, k_cols_from, head_dim):
    j = pl.program_id(1)
    tn = o_ref.shape[1]

    @pl.when(j == 0)
    def _():
        xb_ref[...] = x_ref[...].astype(BF16)

    y = jnp.dot(xb_ref[...], w_ref[...].astype(BF16), preferred_element_type=F32)

    @pl.when(j * tn < rope_cols)
    def _():
        half = head_dim // 2
        cos, sin = cos_ref[...], sin_ref[...]
        scale = jnp.where(j * tn >= k_cols_from, head_dim ** -0.5, 1.0).astype(F32)
        parts = []
        for h in range(tn // head_dim):
            a1 = y[:, h * head_dim:h * head_dim + half]
            a2 = y[:, h * head_dim + half:(h + 1) * head_dim]
            parts.append((a1 * cos - a2 * sin) * scale)
            parts.append((a1 * sin + a2 * cos) * scale)
        o_ref[...] = jnp.concatenate(parts, axis=-1).astype(o_ref.dtype)

    @pl.when(j * tn >= rope_cols)
    def _():
        o_ref[...] = y.astype(o_ref.dtype)


def _project(x, w_in, cos, sin, *, layer, col_start, n_cols, rope_cols, head_dim, out_dtype):
    t, d = x.shape
    seq = cos.shape[0]
    tm, tn = min(PROJ_ROWS, seq), PROJ_COLS
    assert t % tm == 0 and seq % tm == 0 and n_cols % tn == 0 and col_start % tn == 0
    assert tn % head_dim == 0 and rope_cols % tn == 0 and cos.shape[1] == head_dim // 2
    col0 = col_start // tn
    seq_tiles = seq // tm
    return pl.pallas_call(
        functools.partial(_proj_kernel, rope_cols=rope_cols, k_cols_from=rope_cols // 2, head_dim=head_dim),
        grid=(t // tm, n_cols // tn),
        in_specs=[
            pl.BlockSpec((tm, d), lambda i, j: (i, 0)),
            pl.BlockSpec((None, d, tn), lambda i, j: (layer, 0, col0 + j)),
            pl.BlockSpec((tm, head_dim // 2), lambda i, j: (i % seq_tiles, 0)),
            pl.BlockSpec((tm, head_dim // 2), lambda i, j: (i % seq_tiles, 0)),
        ],
        out_specs=pl.BlockSpec((tm, tn), lambda i, j: (i, j)),
        out_shape=jax.ShapeDtypeStruct((t, n_cols), out_dtype),
        scratch_shapes=[pltpu.VMEM((tm, d), BF16)],
        compiler_params=_params("parallel", "arbitrary"),
        name="ret_proj",
    )(x, w_in, cos, sin)


def _retention_kernel(q_ref, k_ref, v_ref, g_ref, o_ref, state_ref, decay_ref, q_decay_ref, k_decay_ref):
    blk = pl.program_id(1)
    length = q_ref.shape[1]
    dk = q_ref.shape[2] // RET_HEADS
    dv = v_ref.shape[2] // RET_HEADS

    @pl.when(blk == 0)
    def _():
        state_ref[...] = jnp.zeros_like(state_ref)
        row = lax.broadcasted_iota(jnp.int32, (length, length), 0)
        col = lax.broadcasted_iota(jnp.int32, (length, length), 1)
        dist = jnp.abs(row - col).astype(F32)
        visible = (col // CHUNK) <= (row // CHUNK)
        pos = lax.broadcasted_iota(jnp.int32, (length, dk), 0).astype(F32)
        for h in range(RET_HEADS):
            lg = LOG_GAMMA[h]
            decay_ref[h] = jnp.where(visible, jnp.exp(dist * lg), 0.0)
            q_decay_ref[h] = jnp.exp((pos + 1.0) * lg)
            k_decay_ref[h] = jnp.exp((length - 1.0 - pos) * lg)

    nt = (((1,), (1,)), ((), ()))
    tn = (((0,), (0,)), ((), ()))
    for h in range(RET_HEADS):
        lg = LOG_GAMMA[h]
        q = q_ref[0, :, h * dk:(h + 1) * dk]
        k = k_ref[0, :, h * dk:(h + 1) * dk]
        v = v_ref[0, :, h * dv:(h + 1) * dv]
        scores = lax.dot_general(q, k, nt, preferred_element_type=F32) * decay_ref[h]
        q_in = (q.astype(F32) * q_decay_ref[h]).astype(BF16)
        k_out = (k.astype(F32) * k_decay_ref[h]).astype(BF16)
        state = state_ref[h]
        out = (jnp.dot(scores.astype(BF16), v, preferred_element_type=F32)
               + jnp.dot(q_in, state.astype(BF16), preferred_element_type=F32))
        state_ref[h] = math.exp(length * lg) * state + lax.dot_general(k_out, v, tn, preferred_element_type=F32)
        mu = jnp.mean(out, axis=-1, keepdims=True)
        oc = out - mu
        var = jnp.mean(oc * oc, axis=-1, keepdims=True)
        normed = oc * lax.rsqrt(var + GN_EPS)
        gate = g_ref[0, :, h * dv:(h + 1) * dv]
        o_ref[0, :, h * dv:(h + 1) * dv] = (jax.nn.silu(gate) * normed).astype(o_ref.dtype)


def _retention_core(qkv, gate, *, batch):
    t, v_dim = gate.shape
    qk_dim = (qkv.shape[1] - v_dim) // 2
    seq = t // batch
    length = min(RET_BLOCK, seq)
    assert seq % length == 0 and length % CHUNK == 0 and v_dim % qk_dim == 0
    qkv3 = qkv.reshape(batch, seq, qkv.shape[1])
    gate3 = gate.reshape(batch, seq, v_dim)
    v_block = 2 * qk_dim // v_dim
    assert v_block * v_dim == 2 * qk_dim
    out = pl.pallas_call(
        _retention_kernel,
        grid=(batch, seq // length),
        in_specs=[
            pl.BlockSpec((1, length, qk_dim), lambda b, s: (b, s, 0)),
            pl.BlockSpec((1, length, qk_dim), lambda b, s: (b, s, 1)),
            pl.BlockSpec((1, length, v_dim), lambda b, s: (b, s, v_block)),
            pl.BlockSpec((1, length, v_dim), lambda b, s: (b, s, 0)),
        ],
        out_specs=pl.BlockSpec((1, length, v_dim), lambda b, s: (b, s, 0)),
        out_shape=jax.ShapeDtypeStruct((batch, seq, v_dim), BF16),
        scratch_shapes=[pltpu.VMEM((RET_HEADS, qk_dim // RET_HEADS, v_dim // RET_HEADS), F32),
                        pltpu.VMEM((RET_HEADS, length, length), F32),
                        pltpu.VMEM((RET_HEADS, length, qk_dim // RET_HEADS), F32),
                        pltpu.VMEM((RET_HEADS, length, qk_dim // RET_HEADS), F32)],
        compiler_params=_params("parallel", "arbitrary"),
        name="ret_core",
    )(qkv3, qkv3, qkv3, gate3)
    return out.reshape(t, v_dim)


def _pack_bf16_pairs(x):
    half = x.shape[1] // 2
    lo = lax.bitcast_convert_type(x[:, :half].astype(BF16).astype(F32), jnp.uint32)
    hi = lax.bitcast_convert_type(x[:, half:].astype(BF16).astype(F32), jnp.uint32)
    return hi | (lo >> 16)


def _unpack_bf16_pairs(xp):
    lo = lax.bitcast_convert_type(xp << 16, F32).astype(BF16)
    hi = lax.bitcast_convert_type(xp & jnp.uint32(0xFFFF0000), F32).astype(BF16)
    return jnp.concatenate([lo, hi], axis=1)


def _outproj_ln_router_kernel(o_ref, wo_ref, x_ref, gain_ref, bias_ref, wr_ref,
                              x1_ref, xp_ref, idx_ref, gates_ref, *, ln_row):
    kk = pl.program_id(1)

    @pl.when(kk == 0)
    def _():
        x1_ref[...] = jnp.zeros_like(x1_ref)

    x1_ref[...] += jnp.dot(o_ref[...], wo_ref[...].astype(BF16), preferred_element_type=F32)

    @pl.when(kk == pl.num_programs(1) - 1)
    def _():
        x1 = _layer_norm(ALPHA * x_ref[...] + x1_ref[...], _row(gain_ref, ln_row), _row(bias_ref, ln_row))
        x1_ref[...] = x1
        xp_ref[...] = _pack_bf16_pairs(x1)
        x_hi = x1.astype(BF16)
        x_lo = (x1 - x_hi.astype(F32)).astype(BF16)
        w = wr_ref[...]
        w_hi = w.astype(BF16)
        w_lo = (w - w_hi.astype(F32)).astype(BF16)
        logits = (jnp.dot(x_hi, w_hi, preferred_element_type=F32)
                  + (jnp.dot(x_hi, w_lo, preferred_element_type=F32)
                     + jnp.dot(x_lo, w_hi, preferred_element_type=F32)))
        expert = lax.broadcasted_iota(jnp.int32, logits.shape, 1).astype(F32)
        none = float(N_EXPERTS)
        m1 = jnp.max(logits, axis=1, keepdims=True)
        i1 = jnp.min(jnp.where(logits == m1, expert, none), axis=1, keepdims=True)
        rest = jnp.where(expert == i1, -jnp.inf, logits)
        m2 = jnp.max(rest, axis=1, keepdims=True)
        i2 = jnp.min(jnp.where(rest == m2, expert, none), axis=1, keepdims=True)
        p2 = jnp.exp(m2 - m1)
        denom = 1.0 + p2
        first = lax.broadcasted_iota(jnp.int32, idx_ref.shape, 1) == 0
        idx_ref[...] = jnp.where(first, i1, i2).astype(jnp.int32)
        gates_ref[...] = jnp.where(first, 1.0 / denom, p2 / denom)


def _outproj_ln_router(o, w_o, x, gains, biases, w_router, *, layer, ln_row):
    t, d = x.shape
    v_dim = o.shape[1]
    tm, tk = min(OUTPROJ_ROWS, t), min(OUTPROJ_K, v_dim)
    assert t % tm == 0 and v_dim % tk == 0 and TOP_K == 2
    full = lambda a: pl.BlockSpec(a.shape, lambda i, k: (0,) * a.ndim)
    return pl.pallas_call(
        functools.partial(_outproj_ln_router_kernel, ln_row=ln_row),
        grid=(t // tm, v_dim // tk),
        in_specs=[
            pl.BlockSpec((tm, tk), lambda i, k: (i, k)),
            pl.BlockSpec((None, tk, d), lambda i, k: (layer, k, 0)),
            pl.BlockSpec((tm, d), lambda i, k: (i, 0)),
            full(gains), full(biases),
            pl.BlockSpec((None, d, N_EXPERTS), lambda i, k: (layer, 0, 0)),
        ],
        out_specs=[
            pl.BlockSpec((tm, d), lambda i, k: (i, 0)),
            pl.BlockSpec((tm, d // 2), lambda i, k: (i, 0)),
            pl.BlockSpec((tm, TOP_K), lambda i, k: (i, 0)),
            pl.BlockSpec((tm, TOP_K), lambda i, k: (i, 0)),
        ],
        out_shape=[
            jax.ShapeDtypeStruct((t, d), F32),
            jax.ShapeDtypeStruct((t, d // 2), jnp.uint32),
            jax.ShapeDtypeStruct((t, TOP_K), jnp.int32),
            jax.ShapeDtypeStruct((t, TOP_K), F32),
        ],
        compiler_params=_params("parallel", "arbitrary"),
        name="outproj_ln_router",
    )(o, w_o, x, gains, biases, w_router)


def _route_plan(idx, block_rows):
    flat_e = idx.reshape(-1)
    n_assign = flat_e.shape[0]
    experts = jnp.arange(N_EXPERTS, dtype=jnp.int32)
    onehot = (flat_e[:, None] == experts[None, :]).astype(jnp.int32)
    running = jnp.cumsum(onehot, axis=0)
    rank = jnp.sum((running - onehot) * onehot, axis=1)
    counts = running[-1]
    padded = (counts + block_rows - 1) // block_rows * block_rows
    pad_end = jnp.cumsum(padded)
    pad_start = pad_end - padded
    dest = (jnp.sum(onehot * pad_start[None, :], axis=1) + rank).astype(jnp.int32)
    n_blocks = n_assign // block_rows + N_EXPERTS
    block_start = jnp.arange(n_blocks, dtype=jnp.int32) * block_rows
    block_expert = jnp.minimum(jnp.sum(block_start[:, None] >= pad_end[None, :], axis=1), N_EXPERTS - 1)
    live_rows = jnp.clip((pad_start + counts)[block_expert] - block_start, 0, block_rows)
    live_sub = (live_rows + SUB_ROWS - 1) // SUB_ROWS
    return dest, block_expert.astype(jnp.int32), live_sub.astype(jnp.int32), n_blocks


def _for_each_row(n_rows, body):
    assert n_rows % DMA_UNROLL == 0

    def group(g, carry):
        for u in range(DMA_UNROLL):
            body(g * DMA_UNROLL + u)
        return carry

    lax.fori_loop(0, n_rows // DMA_UNROLL, group, 0)


def _dispatch_kernel(dest_ref, x_ref, init_hbm, xs_hbm, sem):
    del init_hbm
    tokens = x_ref.shape[0]
    base = pl.program_id(0) * tokens

    def row_copy(j, k):
        return pltpu.make_async_copy(x_ref.at[pl.ds(j, 1)], xs_hbm.at[pl.ds(dest_ref[TOP_K * (base + j) + k], 1)], sem)

    def issue(j):
        for k in range(TOP_K):
            row_copy(j, k).start()

    def drain(j):
        for k in range(TOP_K):
            row_copy(j, k).wait()

    _for_each_row(tokens, issue)
    _for_each_row(tokens, drain)


def _dispatch(x, dest, n_rows):
    t, d = x.shape
    tokens = min(ROUTE_ROWS, t)
    assert t % tokens == 0
    return pl.pallas_call(
        _dispatch_kernel,
        grid_spec=pltpu.PrefetchScalarGridSpec(
            num_scalar_prefetch=1,
            grid=(t // tokens,),
            in_specs=[pl.BlockSpec((tokens, d), lambda i, dest: (i, 0)), pl.BlockSpec(memory_space=pl.ANY)],
            out_specs=pl.BlockSpec(memory_space=pl.ANY),
            scratch_shapes=[pltpu.SemaphoreType.DMA(())],
        ),
        out_shape=jax.ShapeDtypeStruct((n_rows, d), x.dtype),
        input_output_aliases={2: 0},
        compiler_params=pltpu.CompilerParams(dimension_semantics=("arbitrary",), has_side_effects=True),
        name="moe_dispatch",
    )(dest, x, jnp.zeros((n_rows, d), x.dtype))


def _expert_kernel(block_expert_ref, live_sub_ref, xs_ref, wg_ref, wu_ref, wd_ref, ys_ref,
                   xb_ref, wgb_ref, wub_ref, wdb_ref):
    del block_expert_ref
    i, f = pl.program_id(0), pl.program_id(1)
    live_sub = live_sub_ref[i]

    @pl.when(f == 0)
    def _():
        xb_ref[...] = _unpack_bf16_pairs(xs_ref[...])
        ys_ref[...] = jnp.zeros_like(ys_ref)

    n_sub = xs_ref.shape[0] // SUB_ROWS
    sub_blocks = [pl.ds(sb * SUB_ROWS, SUB_ROWS) for sb in range(n_sub)]

    @pl.when(live_sub == n_sub)
    def _():
        _cast_weights((wg_ref, wu_ref, wd_ref), (wgb_ref, wub_ref, wdb_ref))
        _swiglu_rows(pl.ds(0, xs_ref.shape[0]), xb_ref, wgb_ref, wub_ref, wdb_ref, ys_ref)

    @pl.when(jnp.logical_and(live_sub > 0, live_sub < n_sub))
    def _():
        _cast_weights((wg_ref, wu_ref, wd_ref), (wgb_ref, wub_ref, wdb_ref))
        for sb in range(n_sub - 1):
            @pl.when(sb < live_sub)
            def _():
                _swiglu_rows(sub_blocks[sb], xb_ref, wgb_ref, wub_ref, wdb_ref, ys_ref)


def _experts(xs, block_expert, live_sub, w_gate, w_up, w_down, *, layer, block_rows):
    n_rows = xs.shape[0]
    d, d_ff = w_gate.shape[2], w_gate.shape[3]
    tf = min(MOE_COLS, d_ff)
    assert n_rows % block_rows == 0 and d_ff % tf == 0 and block_rows % SUB_ROWS == 0 and xs.shape[1] * 2 == d
    n_f = d_ff // tf

    def col(i, f, live):
        return jnp.where(live[i] > 0, f, n_f - 1)

    return pl.pallas_call(
        _expert_kernel,
        grid_spec=pltpu.PrefetchScalarGridSpec(
            num_scalar_prefetch=2,
            grid=(n_rows // block_rows, n_f),
            in_specs=[
                pl.BlockSpec((block_rows, d // 2), lambda i, f, be, live: (i, 0)),
                pl.BlockSpec((None, None, d, tf), lambda i, f, be, live: (layer, be[i], 0, col(i, f, live))),
                pl.BlockSpec((None, None, d, tf), lambda i, f, be, live: (layer, be[i], 0, col(i, f, live))),
                pl.BlockSpec((None, None, tf, d), lambda i, f, be, live: (layer, be[i], col(i, f, live), 0)),
            ],
            out_specs=pl.BlockSpec((block_rows, d), lambda i, f, be, live: (i, 0)),
            scratch_shapes=[pltpu.VMEM((block_rows, d), BF16), pltpu.VMEM((d, tf), BF16),
                            pltpu.VMEM((d, tf), BF16), pltpu.VMEM((tf, d), BF16)],
        ),
        out_shape=jax.ShapeDtypeStruct((n_rows, d), F32),
        compiler_params=_params("parallel", "arbitrary"),
        name="moe_experts",
    )(block_expert, live_sub, xs, w_gate, w_up, w_down)


def _combine_ln_kernel(dest_ref, x_ref, gates_ref, gain_ref, bias_ref, ys_hbm, o_ref, y_buf, sems, *, ln_row):
    tokens = x_ref.shape[0]
    tile = pl.program_id(0)

    def row_copy(which, j, k):
        slot = which % 2
        return pltpu.make_async_copy(ys_hbm.at[pl.ds(dest_ref[TOP_K * (which * tokens + j) + k], 1)],
                                     y_buf.at[slot, k, pl.ds(j, 1)], sems.at[slot])

    def gather(which):
        def issue(j):
            for k in range(TOP_K):
                row_copy(which, j, k).start()
        _for_each_row(tokens, issue)

    def drain(j):
        for k in range(TOP_K):
            row_copy(tile, j, k).wait()

    @pl.when(tile == 0)
    def _():
        gather(tile)

    @pl.when(tile + 1 < pl.num_programs(0))
    def _():
        gather(tile + 1)

    _for_each_row(tokens, drain)
    slot = tile % 2
    gates = gates_ref[...]
    mix = gates[:, 0:1] * y_buf[slot, 0] + gates[:, 1:2] * y_buf[slot, 1]
    o_ref[...] = _layer_norm(ALPHA * x_ref[...] + mix, _row(gain_ref, ln_row), _row(bias_ref, ln_row))


def _combine_ln(x, ys, dest, gates, gains, biases, *, ln_row):
    t, d = x.shape
    tokens = min(ROUTE_ROWS, t)
    assert t % tokens == 0
    full = lambda a: pl.BlockSpec(a.shape, lambda i, dest: (0,) * a.ndim)
    return pl.pallas_call(
        functools.partial(_combine_ln_kernel, ln_row=ln_row),
        grid_spec=pltpu.PrefetchScalarGridSpec(
            num_scalar_prefetch=1,
            grid=(t // tokens,),
            in_specs=[
                pl.BlockSpec((tokens, d), lambda i, dest: (i, 0)),
                pl.BlockSpec((tokens, TOP_K), lambda i, dest: (i, 0)),
                full(gains), full(biases),
                pl.BlockSpec(memory_space=pl.ANY),
            ],
            out_specs=pl.BlockSpec((tokens, d), lambda i, dest: (i, 0)),
            scratch_shapes=[pltpu.VMEM((2, TOP_K, tokens, d), F32), pltpu.SemaphoreType.DMA((2,))],
        ),
        out_shape=jax.ShapeDtypeStruct((t, d), F32),
        compiler_params=_params("arbitrary"),
        name="moe_combine_ln",
    )(dest, x, gates, gains, biases, ys)


def _moe_ln(x, x_packed, idx, gates, w_gate, w_up, w_down, gains, biases, *, layer, ln_row):
    dest, block_expert, live_sub, n_blocks = _route_plan(idx, MOE_ROWS)
    xs = _dispatch(x_packed, dest, n_blocks * MOE_ROWS)
    ys = _experts(xs, block_expert, live_sub, w_gate, w_up, w_down, layer=layer, block_rows=MOE_ROWS)
    return _combine_ln(x, ys, dest, gates, gains, biases, ln_row=ln_row)


def _rope_tables(seq, head_dim):
    half = head_dim // 2
    inv_freq = ROPE_BASE ** (-jnp.arange(half, dtype=F32) / half)
    ang = jnp.arange(seq, dtype=F32)[:, None] * inv_freq[None, :]
    return jnp.cos(ang), jnp.sin(ang)


def kernel(x, ln_gain, ln_bias, pool_w, pool_scale, ret_w_in, ret_w_o, ffn_w_gate, ffn_w_up, ffn_w_down,
           moe_w_router, moe_w_gate, moe_w_up, moe_w_down):
    batch, seq, d = x.shape
    t = batch * seq
    gains = ln_gain.reshape(2 * DEPTH, d)
    biases = ln_bias.reshape(2 * DEPTH, d)
    qk_dim = ret_w_o.shape[2]
    v_dim = ret_w_o.shape[1]
    head_dim = qk_dim // RET_HEADS
    cos, sin = _rope_tables(seq, head_dim)
    pool_w = pool_w.astype(BF16)
    ret_w_in = ret_w_in.astype(BF16)
    ret_w_o = ret_w_o.astype(BF16)
    ffn_w_gate, ffn_w_up, ffn_w_down = (w.astype(BF16) for w in (ffn_w_gate, ffn_w_up, ffn_w_down))

    for i in range(DEPTH):
        j = i // 2
        if i % 2 == 0:
            x = _pool_ln(x, pool_w, pool_scale, gains, biases, layer=j, ln_row=2 * i)
            x2 = _swiglu_ln(x.reshape(t, d), ffn_w_gate, ffn_w_up, ffn_w_down, gains, biases,
                            layer=j, ln_row=2 * i + 1)
        else:
            x2 = x.reshape(t, d)
            qkv = _project(x2, ret_w_in, cos, sin, layer=j, col_start=0, n_cols=2 * qk_dim + v_dim,
                           rope_cols=2 * qk_dim, head_dim=head_dim, out_dtype=BF16)
            gate = _project(x2, ret_w_in, cos, sin, layer=j, col_start=2 * qk_dim + v_dim, n_cols=v_dim,
                            rope_cols=0, head_dim=head_dim, out_dtype=F32)
            o = _retention_core(qkv, gate, batch=batch)
            x1, x1_packed, idx, gates = _outproj_ln_router(o, ret_w_o, x2, gains, biases, moe_w_router,
                                                           layer=j, ln_row=2 * i)
            x2 = _moe_ln(x1, x1_packed, idx, gates, moe_w_gate, moe_w_up, moe_w_down, gains, biases,
                         layer=j, ln_row=2 * i + 1)
        x = x2.reshape(batch, seq, d)
    return x
```

```python
import functools
import math

import jax
import jax.numpy as jnp
from jax import lax
from jax.experimental import pallas as pl
from jax.experimental.pallas import tpu as pltpu

F32 = jnp.float32
BF16 = jnp.bfloat16

DEPTH = 4
POOL_WINDOWS = (2, 4, 8, 16)
RET_HEADS = 8
CHUNK = 64
ROPE_BASE = 10000.0
N_EXPERTS = 8
TOP_K = 2
LN_EPS = 1e-5
GN_EPS = 1e-6
ALPHA = (2 * DEPTH) ** 0.25
LOG_GAMMA = tuple(math.log1p(-(2.0 ** (-5 - h))) for h in range(RET_HEADS))

V7X_VMEM_BYTES = 64 * 1024 * 1024
V7X_F32_SUBLANES = 8
POOL_HALO = 2 * V7X_F32_SUBLANES

VMEM_LIMIT = V7X_VMEM_BYTES - 8 * 1024 * 1024
POOL_ROWS = 512
PROJ_ROWS, PROJ_COLS = 1024, 1024
RET_BLOCK = 256
OUTPROJ_ROWS, OUTPROJ_K = 512, 2048
FFN_ROWS, FFN_COLS = 512, 512
MOE_ROWS, MOE_COLS = 1024, 256
SUB_ROWS = 256
ROUTE_ROWS = 512


def _params(*semantics):
    return pltpu.CompilerParams(dimension_semantics=semantics, vmem_limit_bytes=VMEM_LIMIT)


def _layer_norm(z, gain, bias):
    mu = jnp.mean(z, axis=-1, keepdims=True)
    zc = z - mu
    var = jnp.mean(zc * zc, axis=-1, keepdims=True)
    return zc * lax.rsqrt(var + LN_EPS) * gain + bias


def _row(ref, r):
    return ref[r:r + 1, :]


def _pool_ln_kernel(x_ref, halo_ref, w_ref, scale_ref, gain_ref, bias_ref, o_ref, *, layer, ln_row):
    ts, d = x_ref.shape[1], x_ref.shape[2]
    group = d // len(POOL_WINDOWS)
    s = pl.program_id(1)
    x = x_ref[0]
    halo = jnp.where(s > 0, halo_ref[0], 0.0)
    ext = jnp.concatenate([halo, x], axis=0)
    seen = (s * ts + 1 + lax.broadcasted_iota(jnp.int32, (ts, 1), 0)).astype(F32)
    mixed = []
    for g, win in enumerate(POOL_WINDOWS):
        cols = slice(g * group, (g + 1) * group)
        acc = ext[:, cols]
        span = 1
        while span < win:
            acc = acc + pltpu.roll(acc, span, 0)
            span *= 2
        pooled = acc[POOL_HALO:] * (1.0 / jnp.minimum(seen, float(win)))
        diff = (pooled - x[:, cols]).astype(BF16)
        mixed.append(jnp.dot(diff, w_ref[g].astype(BF16), preferred_element_type=F32))
    mix = jnp.concatenate(mixed, axis=-1) * _row(scale_ref, layer)
    o_ref[0] = _layer_norm(ALPHA * x + mix, _row(gain_ref, ln_row), _row(bias_ref, ln_row))


def _pool_ln(x, pool_w, pool_scale, gains, biases, *, layer, ln_row):
    b, s, d = x.shape
    ts = min(POOL_ROWS, s)
    assert s % ts == 0 and ts % POOL_HALO == 0 and max(POOL_WINDOWS) <= POOL_HALO
    halo_blocks = ts // POOL_HALO
    n_groups, group = pool_w.shape[1], pool_w.shape[2]
    full = lambda a: pl.BlockSpec(a.shape, lambda bi, si: (0,) * a.ndim)
    return pl.pallas_call(
        functools.partial(_pool_ln_kernel, layer=layer, ln_row=ln_row),
        grid=(b, s // ts),
        in_specs=[
            pl.BlockSpec((1, ts, d), lambda bi, si: (bi, si, 0)),
            pl.BlockSpec((1, POOL_HALO, d), lambda bi, si: (bi, jnp.maximum(si * halo_blocks - 1, 0), 0)),
            pl.BlockSpec((None, n_groups, group, group), lambda bi, si: (layer, 0, 0, 0)),
            full(pool_scale), full(gains), full(biases),
        ],
        out_specs=pl.BlockSpec((1, ts, d), lambda bi, si: (bi, si, 0)),
        out_shape=jax.ShapeDtypeStruct(x.shape, F32),
        compiler_params=_params("parallel", "parallel"),
        name="pool_ln",
    )(x, x, pool_w, pool_scale, gains, biases)


def _cast_weights(w_refs, wb_refs):
    for w_ref, wb_ref in zip(w_refs, wb_refs):
        wb_ref[...] = w_ref[...].astype(BF16)


def _swiglu_rows(rows, xb_ref, wgb_ref, wub_ref, wdb_ref, acc_ref):
    xb = xb_ref[rows, :]
    gate = jnp.dot(xb, wgb_ref[...], preferred_element_type=F32)
    up = jnp.dot(xb, wub_ref[...], preferred_element_type=F32)
    hidden = (jax.nn.silu(gate) * up).astype(BF16)
    acc_ref[rows, :] += jnp.dot(hidden, wdb_ref[...], preferred_element_type=F32)


def _swiglu_ln_kernel(x_ref, wg_ref, wu_ref, wd_ref, gain_ref, bias_ref, o_ref, xb_ref, *, ln_row):
    f = pl.program_id(1)
    sub_blocks = [pl.ds(r, SUB_ROWS) for r in range(0, x_ref.shape[0], SUB_ROWS)]

    @pl.when(f == 0)
    def _():
        xb_ref[...] = x_ref[...].astype(BF16)
        o_ref[...] = jnp.zeros_like(o_ref)

    _swiglu_rows(pl.ds(0, x_ref.shape[0]), xb_ref, wg_ref, wu_ref, wd_ref, o_ref)

    @pl.when(f == pl.num_programs(1) - 1)
    def _():
        for rows in sub_blocks:
            o_ref[rows, :] = _layer_norm(ALPHA * x_ref[rows, :] + o_ref[rows, :],
                                         _row(gain_ref, ln_row), _row(bias_ref, ln_row))


def _swiglu_ln(x, w_gate, w_up, w_down, gains, biases, *, layer, ln_row):
    t, d = x.shape
    d_ff = w_gate.shape[2]
    tm, tf = min(FFN_ROWS, t), min(FFN_COLS, d_ff)
    assert t % tm == 0 and d_ff % tf == 0 and tm % SUB_ROWS == 0
    full = lambda a: pl.BlockSpec(a.shape, lambda i, f: (0,) * a.ndim)
    return pl.pallas_call(
        functools.partial(_swiglu_ln_kernel, ln_row=ln_row),
        grid=(t // tm, d_ff // tf),
        in_specs=[
            pl.BlockSpec((tm, d), lambda i, f: (i, 0)),
            pl.BlockSpec((None, d, tf), lambda i, f: (layer, 0, f)),
            pl.BlockSpec((None, d, tf), lambda i, f: (layer, 0, f)),
            pl.BlockSpec((None, tf, d), lambda i, f: (layer, f, 0)),
            full(gains), full(biases),
        ],
        out_specs=pl.BlockSpec((tm, d), lambda i, f: (i, 0)),
        out_shape=jax.ShapeDtypeStruct((t, d), F32),
        scratch_shapes=[pltpu.VMEM((tm, d), BF16)],
        compiler_params=_params("parallel", "arbitrary"),
        name="swiglu_ln",
    )(x, w_gate, w_up, w_down, gains, biases)


def _proj_kernel(x_ref, w_ref, cos_ref, sin_ref, o_ref, xb_ref, *, rope_cols, k_cols_from, head_dim):
    j = pl.program_id(1)
    tn = o_ref.shape[1]

    @pl.when(j == 0)
    def _():
        xb_ref[...] = x_ref[...].astype(BF16)

    y = jnp.dot(xb_ref[...], w_ref[...].astype(BF16), preferred_element_type=F32)

    @pl.when(j * tn < rope_cols)
    def _():
        half = head_dim // 2
        cos, sin = cos_ref[...], sin_ref[...]
        scale = jnp.where(j * tn >= k_cols_from, head_dim ** -0.5, 1.0).astype(F32)
        parts = []
        for h in range(tn // head_dim):
            a1 = y[:, h * head_dim:h * head_dim + half]
            a2 = y[:, h * head_dim + half:(h + 1) * head_dim]
            parts.append((a1 * cos - a2 * sin) * scale)
            parts.append((a1 * sin + a2 * cos) * scale)
        o_ref[...] = jnp.concatenate(parts, axis=-1).astype(o_ref.dtype)

    @pl.when(j * tn >= rope_cols)
    def _():
        o_ref[...] = y.astype(o_ref.dtype)


def _project(x, w_in, cos, sin, *, layer, col_start, n_cols, rope_cols, head_dim, out_dtype):
    t, d = x.shape
    seq = cos.shape[0]
    tm, tn = min(PROJ_ROWS, seq), PROJ_COLS
    assert t % tm == 0 and seq % tm == 0 and n_cols % tn == 0 and col_start % tn == 0
    assert tn % head_dim == 0 and rope_cols % tn == 0 and cos.shape[1] == head_dim // 2
    col0 = col_start // tn
    seq_tiles = seq // tm
    return pl.pallas_call(
        functools.partial(_proj_kernel, rope_cols=rope_cols, k_cols_from=rope_cols // 2, head_dim=head_dim),
        grid=(t // tm, n_cols // tn),
        in_specs=[
            pl.BlockSpec((tm, d), lambda i, j: (i, 0)),
            pl.BlockSpec((None, d, tn), lambda i, j: (layer, 0, col0 + j)),
            pl.BlockSpec((tm, head_dim // 2), lambda i, j: (i % seq_tiles, 0)),
            pl.BlockSpec((tm, head_dim // 2), lambda i, j: (i % seq_tiles, 0)),
        ],
        out_specs=pl.BlockSpec((tm, tn), lambda i, j: (i, j)),
        out_shape=jax.ShapeDtypeStruct((t, n_cols), out_dtype),
        scratch_shapes=[pltpu.VMEM((tm, d), BF16)],
        compiler_params=_params("parallel", "arbitrary"),
        name="ret_proj",
    )(x, w_in, cos, sin)


def _retention_kernel(q_ref, k_ref, v_ref, x_ref, wg_ref, o_ref, state_ref, decay_ref, q_decay_ref, k_decay_ref):
    blk = pl.program_id(1)
    length = q_ref.shape[1]
    dk = q_ref.shape[2] // RET_HEADS
    dv = v_ref.shape[2] // RET_HEADS

    @pl.when(blk == 0)
    def _():
        state_ref[...] = jnp.zeros_like(state_ref)
        row = lax.broadcasted_iota(jnp.int32, (length, length), 0)
        col = lax.broadcasted_iota(jnp.int32, (length, length), 1)
        dist = jnp.abs(row - col).astype(F32)
        visible = (col // CHUNK) <= (row // CHUNK)
        pos = lax.broadcasted_iota(jnp.int32, (length, dk), 0).astype(F32)
        for h in range(RET_HEADS):
            lg = LOG_GAMMA[h]
            decay_ref[h] = jnp.where(visible, jnp.exp(dist * lg), 0.0)
            q_decay_ref[h] = jnp.exp((pos + 1.0) * lg)
            k_decay_ref[h] = jnp.exp((length - 1.0 - pos) * lg)

    nt = (((1,), (1,)), ((), ()))
    tn = (((0,), (0,)), ((), ()))
    xb = x_ref[0].astype(BF16)
    for h in range(RET_HEADS):
        lg = LOG_GAMMA[h]
        q = q_ref[0, :, h * dk:(h + 1) * dk]
        k = k_ref[0, :, h * dk:(h + 1) * dk]
        v = v_ref[0, :, h * dv:(h + 1) * dv]
        scores = lax.dot_general(q, k, nt, preferred_element_type=F32) * decay_ref[h]
        q_in = (q.astype(F32) * q_decay_ref[h]).astype(BF16)
        k_out = (k.astype(F32) * k_decay_ref[h]).astype(BF16)
        state = state_ref[h]
        out = (jnp.dot(scores.astype(BF16), v, preferred_element_type=F32)
               + jnp.dot(q_in, state.astype(BF16), preferred_element_type=F32))
        state_ref[h] = math.exp(length * lg) * state + lax.dot_general(k_out, v, tn, preferred_element_type=F32)
        mu = jnp.mean(out, axis=-1, keepdims=True)
        oc = out - mu
        var = jnp.mean(oc * oc, axis=-1, keepdims=True)
        normed = oc * lax.rsqrt(var + GN_EPS)
        gate = jnp.dot(xb, wg_ref[:, h * dv:(h + 1) * dv], preferred_element_type=F32)
        o_ref[0, :, h * dv:(h + 1) * dv] = (jax.nn.silu(gate) * normed).astype(o_ref.dtype)


def _retention_core(qkv, x, w_in, *, layer, batch):
    t, d = x.shape
    v_dim = w_in.shape[2] - qkv.shape[1]
    qk_dim = (qkv.shape[1] - v_dim) // 2
    seq = t // batch
    length = min(RET_BLOCK, seq)
    assert seq % length == 0 and length % CHUNK == 0
    qkv3 = qkv.reshape(batch, seq, qkv.shape[1])
    v_block = 2 * qk_dim // v_dim
    gate_block = qkv.shape[1] // v_dim
    assert v_block * v_dim == 2 * qk_dim and gate_block * v_dim == qkv.shape[1]
    out = pl.pallas_call(
        _retention_kernel,
        grid=(batch, seq // length),
        in_specs=[
            pl.BlockSpec((1, length, qk_dim), lambda b, s: (b, s, 0)),
            pl.BlockSpec((1, length, qk_dim), lambda b, s: (b, s, 1)),
            pl.BlockSpec((1, length, v_dim), lambda b, s: (b, s, v_block)),
            pl.BlockSpec((1, length, d), lambda b, s: (b, s, 0)),
            pl.BlockSpec((None, d, v_dim), lambda b, s: (layer, 0, gate_block), pipeline_mode=pl.Buffered(1)),
        ],
        out_specs=pl.BlockSpec((1, length, v_dim), lambda b, s: (b, s, 0)),
        out_shape=jax.ShapeDtypeStruct((batch, seq, v_dim), BF16),
        scratch_shapes=[pltpu.VMEM((RET_HEADS, qk_dim // RET_HEADS, v_dim // RET_HEADS), F32),
                        pltpu.VMEM((RET_HEADS, length, length), F32),
                        pltpu.VMEM((RET_HEADS, length, qk_dim // RET_HEADS), F32),
                        pltpu.VMEM((RET_HEADS, length, qk_dim // RET_HEADS), F32)],
        compiler_params=_params("parallel", "arbitrary"),
        name="ret_core",
    )(qkv3, qkv3, qkv3, x.reshape(batch, seq, d), w_in)
    return out.reshape(t, v_dim)


def _pack_bf16_pairs(x):
    half = x.shape[1] // 2
    lo = lax.bitcast_convert_type(x[:, :half].astype(BF16).astype(F32), jnp.uint32)
    hi = lax.bitcast_convert_type(x[:, half:].astype(BF16).astype(F32), jnp.uint32)
    return hi | (lo >> 16)


def _unpack_bf16_pairs(xp):
    lo = lax.bitcast_convert_type(xp << 16, F32).astype(BF16)
    hi = lax.bitcast_convert_type(xp & jnp.uint32(0xFFFF0000), F32).astype(BF16)
    return jnp.concatenate([lo, hi], axis=1)


def _outproj_ln_router_kernel(o_ref, wo_ref, x_ref, gain_ref, bias_ref, wr_ref,
                              x1_ref, xp_ref, idx_ref, gates_ref, *, ln_row):
    kk = pl.program_id(1)

    @pl.when(kk == 0)
    def _():
        x1_ref[...] = jnp.zeros_like(x1_ref)

    x1_ref[...] += jnp.dot(o_ref[...], wo_ref[...].astype(BF16), preferred_element_type=F32)

    @pl.when(kk == pl.num_programs(1) - 1)
    def _():
        x1 = _layer_norm(ALPHA * x_ref[...] + x1_ref[...], _row(gain_ref, ln_row), _row(bias_ref, ln_row))
        x1_ref[...] = x1
        xp_ref[...] = _pack_bf16_pairs(x1)
        x_hi = x1.astype(BF16)
        x_lo = (x1 - x_hi.astype(F32)).astype(BF16)
        w = wr_ref[...]
        w_hi = w.astype(BF16)
        w_lo = (w - w_hi.astype(F32)).astype(BF16)
        logits = (jnp.dot(x_hi, w_hi, preferred_element_type=F32)
                  + (jnp.dot(x_hi, w_lo, preferred_element_type=F32)
                     + jnp.dot(x_lo, w_hi, preferred_element_type=F32)))
        expert = lax.broadcasted_iota(jnp.int32, logits.shape, 1).astype(F32)
        none = float(N_EXPERTS)
        m1 = jnp.max(logits, axis=1, keepdims=True)
        i1 = jnp.min(jnp.where(logits == m1, expert, none), axis=1, keepdims=True)
        rest = jnp.where(expert == i1, -jnp.inf, logits)
        m2 = jnp.max(rest, axis=1, keepdims=True)
        i2 = jnp.min(jnp.where(rest == m2, expert, none), axis=1, keepdims=True)
        p2 = jnp.exp(m2 - m1)
        denom = 1.0 + p2
        first = lax.broadcasted_iota(jnp.int32, idx_ref.shape, 1) == 0
        idx_ref[...] = jnp.where(first, i1, i2).astype(jnp.int32)
        gates_ref[...] = jnp.where(first, 1.0 / denom, p2 / denom)


def _outproj_ln_router(o, w_o, x, gains, biases, w_router, *, layer, ln_row):
    t, d = x.shape
    v_dim = o.shape[1]
    tm, tk = min(OUTPROJ_ROWS, t), min(OUTPROJ_K, v_dim)
    assert t % tm == 0 and v_dim % tk == 0 and TOP_K == 2
    full = lambda a: pl.BlockSpec(a.shape, lambda i, k: (0,) * a.ndim)
    return pl.pallas_call(
        functools.partial(_outproj_ln_router_kernel, ln_row=ln_row),
        grid=(t // tm, v_dim // tk),
        in_specs=[
            pl.BlockSpec((tm, tk), lambda i, k: (i, k)),
            pl.BlockSpec((None, tk, d), lambda i, k: (layer, k, 0)),
            pl.BlockSpec((tm, d), lambda i, k: (i, 0)),
            full(gains), full(biases),
            pl.BlockSpec((None, d, N_EXPERTS), lambda i, k: (layer, 0, 0)),
        ],
        out_specs=[
            pl.BlockSpec((tm, d), lambda i, k: (i, 0)),
            pl.BlockSpec((tm, d // 2), lambda i, k: (i, 0)),
            pl.BlockSpec((tm, TOP_K), lambda i, k: (i, 0)),
            pl.BlockSpec((tm, TOP_K), lambda i, k: (i, 0)),
        ],
        out_shape=[
            jax.ShapeDtypeStruct((t, d), F32),
            jax.ShapeDtypeStruct((t, d // 2), jnp.uint32),
            jax.ShapeDtypeStruct((t, TOP_K), jnp.int32),
            jax.ShapeDtypeStruct((t, TOP_K), F32),
        ],
        compiler_params=_params("parallel", "arbitrary"),
        name="outproj_ln_router",
    )(o, w_o, x, gains, biases, w_router)


def _route_plan(idx, block_rows):
    flat_e = idx.reshape(-1)
    n_assign = flat_e.shape[0]
    experts = jnp.arange(N_EXPERTS, dtype=jnp.int32)
    onehot = (flat_e[:, None] == experts[None, :]).astype(jnp.int32)
    running = jnp.cumsum(onehot, axis=0)
    rank = jnp.sum((running - onehot) * onehot, axis=1)
    counts = running[-1]
    padded = (counts + block_rows - 1) // block_rows * block_rows
    pad_end = jnp.cumsum(padded)
    pad_start = pad_end - padded
    dest = (jnp.sum(onehot * pad_start[None, :], axis=1) + rank).astype(jnp.int32)
    n_blocks = n_assign // block_rows + N_EXPERTS
    block_start = jnp.arange(n_blocks, dtype=jnp.int32) * block_rows
    block_expert = jnp.minimum(jnp.sum(block_start[:, None] >= pad_end[None, :], axis=1), N_EXPERTS - 1)
    live_rows = jnp.clip((pad_start + counts)[block_expert] - block_start, 0, block_rows)
    live_sub = (live_rows + SUB_ROWS - 1) // SUB_ROWS
    return dest, block_expert.astype(jnp.int32), live_sub.astype(jnp.int32), n_blocks


def _for_each_sublane_group(n_groups, body):
    lax.fori_loop(0, n_groups, lambda g, carry: (body(g), carry)[1], 0)


def _dispatch_kernel(dest_ref, x_ref, init_hbm, xs_hbm, sem):
    del init_hbm
    groups, sub = x_ref.shape[0], x_ref.shape[1]
    first = pl.program_id(0) * (groups * sub * TOP_K)

    def row_copies(g):
        at = first + g * (sub * TOP_K)
        return [pltpu.make_async_copy(x_ref.at[g, pl.ds(u, 1)], xs_hbm.at[pl.ds(dest_ref[at + TOP_K * u + k], 1)], sem)
                for u in range(sub) for k in range(TOP_K)]

    _for_each_sublane_group(groups, lambda g: [c.start() for c in row_copies(g)])
    _for_each_sublane_group(groups, lambda g: [c.wait() for c in row_copies(g)])


def _dispatch(x, dest, n_rows):
    t, d = x.shape
    tokens = min(ROUTE_ROWS, t)
    sub = V7X_F32_SUBLANES
    assert t % tokens == 0 and tokens % sub == 0
    return pl.pallas_call(
        _dispatch_kernel,
        grid_spec=pltpu.PrefetchScalarGridSpec(
            num_scalar_prefetch=1,
            grid=(t // tokens,),
            in_specs=[pl.BlockSpec((tokens // sub, sub, d), lambda i, dest: (i, 0, 0)),
                      pl.BlockSpec(memory_space=pl.ANY)],
            out_specs=pl.BlockSpec(memory_space=pl.ANY),
            scratch_shapes=[pltpu.SemaphoreType.DMA(())],
        ),
        out_shape=jax.ShapeDtypeStruct((n_rows, d), x.dtype),
        input_output_aliases={2: 0},
        compiler_params=pltpu.CompilerParams(dimension_semantics=("arbitrary",), has_side_effects=True),
        name="moe_dispatch",
    )(dest, x.reshape(t // sub, sub, d), jnp.zeros((n_rows, d), x.dtype))


def _expert_kernel(block_expert_ref, live_sub_ref, xs_ref, wg_ref, wu_ref, wd_ref, ys_ref,
                   xb_ref, wgb_ref, wub_ref, wdb_ref):
    del block_expert_ref
    i, f = pl.program_id(0), pl.program_id(1)
    live_sub = live_sub_ref[i]

    @pl.when(f == 0)
    def _():
        xb_ref[...] = _unpack_bf16_pairs(xs_ref[...])
        ys_ref[...] = jnp.zeros_like(ys_ref)

    n_sub = xs_ref.shape[0] // SUB_ROWS
    sub_blocks = [pl.ds(sb * SUB_ROWS, SUB_ROWS) for sb in range(n_sub)]

    @pl.when(live_sub == n_sub)
    def _():
        _cast_weights((wg_ref, wu_ref, wd_ref), (wgb_ref, wub_ref, wdb_ref))
        _swiglu_rows(pl.ds(0, xs_ref.shape[0]), xb_ref, wgb_ref, wub_ref, wdb_ref, ys_ref)

    @pl.when(jnp.logical_and(live_sub > 0, live_sub < n_sub))
    def _():
        _cast_weights((wg_ref, wu_ref, wd_ref), (wgb_ref, wub_ref, wdb_ref))
        for sb in range(n_sub - 1):
            @pl.when(sb < live_sub)
            def _():
                _swiglu_rows(sub_blocks[sb], xb_ref, wgb_ref, wub_ref, wdb_ref, ys_ref)


def _experts(xs, block_expert, live_sub, w_gate, w_up, w_down, *, layer, block_rows):
    n_rows = xs.shape[0]
    d, d_ff = w_gate.shape[2], w_gate.shape[3]
    tf = min(MOE_COLS, d_ff)
    assert n_rows % block_rows == 0 and d_ff % tf == 0 and block_rows % SUB_ROWS == 0 and xs.shape[1] * 2 == d
    n_f = d_ff // tf

    def col(i, f, live):
        return jnp.where(live[i] > 0, f, n_f - 1)

    return pl.pallas_call(
        _expert_kernel,
        grid_spec=pltpu.PrefetchScalarGridSpec(
            num_scalar_prefetch=2,
            grid=(n_rows // block_rows, n_f),
            in_specs=[
                pl.BlockSpec((block_rows, d // 2), lambda i, f, be, live: (i, 0)),
                pl.BlockSpec((None, None, d, tf), lambda i, f, be, live: (layer, be[i], 0, col(i, f, live))),
                pl.BlockSpec((None, None, d, tf), lambda i, f, be, live: (layer, be[i], 0, col(i, f, live))),
                pl.BlockSpec((None, None, tf, d), lambda i, f, be, live: (layer, be[i], col(i, f, live), 0)),
            ],
            out_specs=pl.BlockSpec((block_rows, d), lambda i, f, be, live: (i, 0)),
            scratch_shapes=[pltpu.VMEM((block_rows, d), BF16), pltpu.VMEM((d, tf), BF16),
                            pltpu.VMEM((d, tf), BF16), pltpu.VMEM((tf, d), BF16)],
        ),
        out_shape=jax.ShapeDtypeStruct((n_rows, d), F32),
        compiler_params=_params("parallel", "arbitrary"),
        name="moe_experts",
    )(block_expert, live_sub, xs, w_gate, w_up, w_down)


def _combine_ln_kernel(dest_ref, x_ref, gates_ref, gain_ref, bias_ref, ys_hbm, o_ref, y_buf, sems, *, ln_row):
    groups, sub = x_ref.shape[0], x_ref.shape[1]
    tile = pl.program_id(0)
    per_tile = groups * sub * TOP_K

    def row_copies(which, g):
        slot = which & 1
        at = which * per_tile + g * (sub * TOP_K)
        return [pltpu.make_async_copy(ys_hbm.at[pl.ds(dest_ref[at + TOP_K * u + k], 1)],
                                      y_buf.at[slot, k, g, pl.ds(u, 1)], sems.at[slot])
                for u in range(sub) for k in range(TOP_K)]

    def gather(which):
        _for_each_sublane_group(groups, lambda g: [c.start() for c in row_copies(which, g)])

    @pl.when(tile == 0)
    def _():
        gather(tile)

    @pl.when(tile + 1 < pl.num_programs(0))
    def _():
        gather(tile + 1)

    _for_each_sublane_group(groups, lambda g: [c.wait() for c in row_copies(tile, g)])
    slot = tile & 1
    gates = gates_ref[...]
    mix = gates[:, :, 0:1] * y_buf[slot, 0] + gates[:, :, 1:2] * y_buf[slot, 1]
    o_ref[...] = _layer_norm(ALPHA * x_ref[...] + mix, _row(gain_ref, ln_row), _row(bias_ref, ln_row))


def _combine_ln(x, ys, dest, gates, gains, biases, *, ln_row):
    t, d = x.shape
    tokens = min(ROUTE_ROWS, t)
    sub = V7X_F32_SUBLANES
    assert t % tokens == 0 and tokens % sub == 0
    groups = tokens // sub
    full = lambda a: pl.BlockSpec(a.shape, lambda i, dest: (0,) * a.ndim)
    out = pl.pallas_call(
        functools.partial(_combine_ln_kernel, ln_row=ln_row),
        grid_spec=pltpu.PrefetchScalarGridSpec(
            num_scalar_prefetch=1,
            grid=(t // tokens,),
            in_specs=[
                pl.BlockSpec((groups, sub, d), lambda i, dest: (i, 0, 0)),
                pl.BlockSpec((groups, sub, TOP_K), lambda i, dest: (i, 0, 0)),
                full(gains), full(biases),
                pl.BlockSpec(memory_space=pl.ANY),
            ],
            out_specs=pl.BlockSpec((groups, sub, d), lambda i, dest: (i, 0, 0)),
            scratch_shapes=[pltpu.VMEM((2, TOP_K, groups, sub, d), F32), pltpu.SemaphoreType.DMA((2,))],
        ),
        out_shape=jax.ShapeDtypeStruct((t // sub, sub, d), F32),
        compiler_params=_params("arbitrary"),
        name="moe_combine_ln",
    )(dest, x.reshape(t // sub, sub, d), gates.reshape(t // sub, sub, TOP_K), gains, biases, ys)
    return out.reshape(t, d)


def _moe_ln(x, x_packed, idx, gates, w_gate, w_up, w_down, gains, biases, *, layer, ln_row):
    dest, block_expert, live_sub, n_blocks = _route_plan(idx, MOE_ROWS)
    xs = _dispatch(x_packed, dest, n_blocks * MOE_ROWS)
    ys = _experts(xs, block_expert, live_sub, w_gate, w_up, w_down, layer=layer, block_rows=MOE_ROWS)
    return _combine_ln(x, ys, dest, gates, gains, biases, ln_row=ln_row)


def _rope_tables(seq, head_dim):
    half = head_dim // 2
    inv_freq = ROPE_BASE ** (-jnp.arange(half, dtype=F32) / half)
    ang = jnp.arange(seq, dtype=F32)[:, None] * inv_freq[None, :]
    return jnp.cos(ang), jnp.sin(ang)


def kernel(x, ln_gain, ln_bias, pool_w, pool_scale, ret_w_in, ret_w_o, ffn_w_gate, ffn_w_up, ffn_w_down,
           moe_w_router, moe_w_gate, moe_w_up, moe_w_down):
    batch, seq, d = x.shape
    t = batch * seq
    gains = ln_gain.reshape(2 * DEPTH, d)
    biases = ln_bias.reshape(2 * DEPTH, d)
    qk_dim = ret_w_o.shape[2]
    v_dim = ret_w_o.shape[1]
    head_dim = qk_dim // RET_HEADS
    cos, sin = _rope_tables(seq, head_dim)
    pool_w = pool_w.astype(BF16)
    ret_w_in = ret_w_in.astype(BF16)
    ret_w_o = ret_w_o.astype(BF16)
    ffn_w_gate, ffn_w_up, ffn_w_down = (w.astype(BF16) for w in (ffn_w_gate, ffn_w_up, ffn_w_down))

    for i in range(DEPTH):
        j = i // 2
        if i % 2 == 0:
            x = _pool_ln(x, pool_w, pool_scale, gains, biases, layer=j, ln_row=2 * i)
            x2 = _swiglu_ln(x.reshape(t, d), ffn_w_gate, ffn_w_up, ffn_w_down, gains, biases,
                            layer=j, ln_row=2 * i + 1)
        else:
            x2 = x.reshape(t, d)
            qkv = _project(x2, ret_w_in, cos, sin, layer=j, col_start=0, n_cols=2 * qk_dim + v_dim,
                           rope_cols=2 * qk_dim, head_dim=head_dim, out_dtype=BF16)
            o = _retention_core(qkv, x2, ret_w_in, layer=j, batch=batch)
            x1, x1_packed, idx, gates = _outproj_ln_router(o, ret_w_o, x2, gains, biases, moe_w_router,
                                                           layer=j, ln_row=2 * i)
            x2 = _moe_ln(x1, x1_packed, idx, gates, moe_w_gate, moe_w_up, moe_w_down, gains, biases,
                         layer=j, ln_row=2 * i + 1)
        x = x2.reshape(batch, seq, d)
    return x
```

```python
import functools
import math

import jax
import jax.numpy as jnp
from jax import lax
from jax.experimental import pallas as pl
from jax.experimental.pallas import tpu as pltpu

F32 = jnp.float32
BF16 = jnp.bfloat16

DEPTH = 4
POOL_WINDOWS = (2, 4, 8, 16)
RET_HEADS = 8
CHUNK = 64
ROPE_BASE = 10000.0
N_EXPERTS = 8
TOP_K = 2
LN_EPS = 1e-5
GN_EPS = 1e-6
ALPHA = (2 * DEPTH) ** 0.25
LOG_GAMMA = tuple(math.log1p(-(2.0 ** (-5 - h))) for h in range(RET_HEADS))

V7X_VMEM_BYTES = 64 * 1024 * 1024
V7X_F32_SUBLANES = 8
POOL_HALO = 2 * V7X_F32_SUBLANES

VMEM_LIMIT = V7X_VMEM_BYTES - 8 * 1024 * 1024
POOL_ROWS = 512
PROJ_ROWS, PROJ_COLS = 1024, 1024
RET_BLOCK = 256
OUTPROJ_ROWS = 512
FFN_ROWS, FFN_COLS = 512, 512
MOE_ROWS, MOE_COLS = 1024, 256
SUB_ROWS = 256
ROUTE_ROWS = 512


def _params(*semantics):
    return pltpu.CompilerParams(dimension_semantics=semantics, vmem_limit_bytes=VMEM_LIMIT)


def _layer_norm(z, gain, bias):
    mu = jnp.mean(z, axis=-1, keepdims=True)
    zc = z - mu
    var = jnp.mean(zc * zc, axis=-1, keepdims=True)
    return zc * lax.rsqrt(var + LN_EPS) * gain + bias


def _row(ref, r):
    return ref[r:r + 1, :]


def _pool_ln_kernel(x_ref, halo_ref, w_ref, scale_ref, gain_ref, bias_ref, o_ref, *, layer, ln_row):
    ts, d = x_ref.shape[1], x_ref.shape[2]
    group = d // len(POOL_WINDOWS)
    s = pl.program_id(1)
    x = x_ref[0]
    halo = jnp.where(s > 0, halo_ref[0], 0.0)
    ext = jnp.concatenate([halo, x], axis=0)
    seen = (s * ts + 1 + lax.broadcasted_iota(jnp.int32, (ts, 1), 0)).astype(F32)
    mixed = []
    for g, win in enumerate(POOL_WINDOWS):
        cols = slice(g * group, (g + 1) * group)
        acc = ext[:, cols]
        span = 1
        while span < win:
            acc = acc + pltpu.roll(acc, span, 0)
            span *= 2
        pooled = acc[POOL_HALO:] * (1.0 / jnp.minimum(seen, float(win)))
        diff = (pooled - x[:, cols]).astype(BF16)
        mixed.append(jnp.dot(diff, w_ref[g].astype(BF16), preferred_element_type=F32))
    mix = jnp.concatenate(mixed, axis=-1) * _row(scale_ref, layer)
    o_ref[0] = _layer_norm(ALPHA * x + mix, _row(gain_ref, ln_row), _row(bias_ref, ln_row))


def _pool_ln(x, pool_w, pool_scale, gains, biases, *, layer, ln_row):
    b, s, d = x.shape
    ts = min(POOL_ROWS, s)
    assert s % ts == 0 and ts % POOL_HALO == 0 and max(POOL_WINDOWS) <= POOL_HALO
    halo_blocks = ts // POOL_HALO
    n_groups, group = pool_w.shape[1], pool_w.shape[2]
    full = lambda a: pl.BlockSpec(a.shape, lambda bi, si: (0,) * a.ndim)
    return pl.pallas_call(
        functools.partial(_pool_ln_kernel, layer=layer, ln_row=ln_row),
        grid=(b, s // ts),
        in_specs=[
            pl.BlockSpec((1, ts, d), lambda bi, si: (bi, si, 0)),
            pl.BlockSpec((1, POOL_HALO, d), lambda bi, si: (bi, jnp.maximum(si * halo_blocks - 1, 0), 0)),
            pl.BlockSpec((None, n_groups, group, group), lambda bi, si: (layer, 0, 0, 0)),
            full(pool_scale), full(gains), full(biases),
        ],
        out_specs=pl.BlockSpec((1, ts, d), lambda bi, si: (bi, si, 0)),
        out_shape=jax.ShapeDtypeStruct(x.shape, F32),
        compiler_params=_params("parallel", "parallel"),
        name="pool_ln",
    )(x, x, pool_w, pool_scale, gains, biases)


def _cast_weights(w_refs, wb_refs):
    for w_ref, wb_ref in zip(w_refs, wb_refs):
        wb_ref[...] = w_ref[...].astype(BF16)


def _swiglu_rows(rows, xb_ref, wgb_ref, wub_ref, wdb_ref, acc_ref):
    xb = xb_ref[rows, :]
    gate = jnp.dot(xb, wgb_ref[...], preferred_element_type=F32)
    up = jnp.dot(xb, wub_ref[...], preferred_element_type=F32)
    hidden = (jax.nn.silu(gate) * up).astype(BF16)
    acc_ref[rows, :] += jnp.dot(hidden, wdb_ref[...], preferred_element_type=F32)


def _swiglu_ln_kernel(x_ref, wg_ref, wu_ref, wd_ref, gain_ref, bias_ref, o_ref, xb_ref, *, ln_row):
    f = pl.program_id(1)
    sub_blocks = [pl.ds(r, SUB_ROWS) for r in range(0, x_ref.shape[0], SUB_ROWS)]

    @pl.when(f == 0)
    def _():
        xb_ref[...] = x_ref[...].astype(BF16)
        o_ref[...] = jnp.zeros_like(o_ref)

    _swiglu_rows(pl.ds(0, x_ref.shape[0]), xb_ref, wg_ref, wu_ref, wd_ref, o_ref)

    @pl.when(f == pl.num_programs(1) - 1)
    def _():
        for rows in sub_blocks:
            o_ref[rows, :] = _layer_norm(ALPHA * x_ref[rows, :] + o_ref[rows, :],
                                         _row(gain_ref, ln_row), _row(bias_ref, ln_row))


def _swiglu_ln(x, w_gate, w_up, w_down, gains, biases, *, layer, ln_row):
    t, d = x.shape
    d_ff = w_gate.shape[2]
    tm, tf = min(FFN_ROWS, t), min(FFN_COLS, d_ff)
    assert t % tm == 0 and d_ff % tf == 0 and tm % SUB_ROWS == 0
    full = lambda a: pl.BlockSpec(a.shape, lambda i, f: (0,) * a.ndim)
    return pl.pallas_call(
        functools.partial(_swiglu_ln_kernel, ln_row=ln_row),
        grid=(t // tm, d_ff // tf),
        in_specs=[
            pl.BlockSpec((tm, d), lambda i, f: (i, 0)),
            pl.BlockSpec((None, d, tf), lambda i, f: (layer, 0, f)),
            pl.BlockSpec((None, d, tf), lambda i, f: (layer, 0, f)),
            pl.BlockSpec((None, tf, d), lambda i, f: (layer, f, 0)),
            full(gains), full(biases),
        ],
        out_specs=pl.BlockSpec((tm, d), lambda i, f: (i, 0)),
        out_shape=jax.ShapeDtypeStruct((t, d), F32),
        scratch_shapes=[pltpu.VMEM((tm, d), BF16)],
        compiler_params=_params("parallel", "arbitrary"),
        name="swiglu_ln",
    )(x, w_gate, w_up, w_down, gains, biases)


def _proj_kernel(x_ref, w_ref, cos_ref, sin_ref, o_ref, xb_ref, *, rope_cols, k_cols_from, head_dim):
    j = pl.program_id(1)
    tn = o_ref.shape[1]

    @pl.when(j == 0)
    def _():
        xb_ref[...] = x_ref[...].astype(BF16)

    y = jnp.dot(xb_ref[...], w_ref[...].astype(BF16), preferred_element_type=F32)

    @pl.when(j * tn < rope_cols)
    def _():
        half = head_dim // 2
        cos, sin = cos_ref[...], sin_ref[...]
        scale = jnp.where(j * tn >= k_cols_from, head_dim ** -0.5, 1.0).astype(F32)
        parts = []
        for h in range(tn // head_dim):
            a1 = y[:, h * head_dim:h * head_dim + half]
            a2 = y[:, h * head_dim + half:(h + 1) * head_dim]
            parts.append((a1 * cos - a2 * sin) * scale)
            parts.append((a1 * sin + a2 * cos) * scale)
        o_ref[...] = jnp.concatenate(parts, axis=-1).astype(o_ref.dtype)

    @pl.when(j * tn >= rope_cols)
    def _():
        o_ref[...] = y.astype(o_ref.dtype)


def _project(x, w_in, cos, sin, *, layer, col_start, n_cols, rope_cols, head_dim, out_dtype):
    t, d = x.shape
    seq = cos.shape[0]
    tm, tn = min(PROJ_ROWS, seq), PROJ_COLS
    assert t % tm == 0 and seq % tm == 0 and n_cols % tn == 0 and col_start % tn == 0
    assert tn % head_dim == 0 and rope_cols % tn == 0 and cos.shape[1] == head_dim // 2
    col0 = col_start // tn
    seq_tiles = seq // tm
    return pl.pallas_call(
        functools.partial(_proj_kernel, rope_cols=rope_cols, k_cols_from=rope_cols // 2, head_dim=head_dim),
        grid=(t // tm, n_cols // tn),
        in_specs=[
            pl.BlockSpec((tm, d), lambda i, j: (i, 0)),
            pl.BlockSpec((None, d, tn), lambda i, j: (layer, 0, col0 + j)),
            pl.BlockSpec((tm, head_dim // 2), lambda i, j: (i % seq_tiles, 0)),
            pl.BlockSpec((tm, head_dim // 2), lambda i, j: (i % seq_tiles, 0)),
        ],
        out_specs=pl.BlockSpec((tm, tn), lambda i, j: (i, j)),
        out_shape=jax.ShapeDtypeStruct((t, n_cols), out_dtype),
        scratch_shapes=[pltpu.VMEM((tm, d), BF16)],
        compiler_params=_params("parallel", "arbitrary"),
        name="ret_proj",
    )(x, w_in, cos, sin)


def _retention_kernel(q_ref, k_ref, v_ref, x_ref, wg_ref, o_ref, state_ref, decay_ref, q_decay_ref, k_decay_ref):
    blk = pl.program_id(1)
    length = q_ref.shape[1]
    dk = q_ref.shape[2] // RET_HEADS
    dv = v_ref.shape[2] // RET_HEADS

    @pl.when(blk == 0)
    def _():
        state_ref[...] = jnp.zeros_like(state_ref)
        row = lax.broadcasted_iota(jnp.int32, (length, length), 0)
        col = lax.broadcasted_iota(jnp.int32, (length, length), 1)
        dist = jnp.abs(row - col).astype(F32)
        visible = (col // CHUNK) <= (row // CHUNK)
        pos = lax.broadcasted_iota(jnp.int32, (length, dk), 0).astype(F32)
        for h in range(RET_HEADS):
            lg = LOG_GAMMA[h]
            decay_ref[h] = jnp.where(visible, jnp.exp(dist * lg), 0.0)
            q_decay_ref[h] = jnp.exp((pos + 1.0) * lg)
            k_decay_ref[h] = jnp.exp((length - 1.0 - pos) * lg)

    nt = (((1,), (1,)), ((), ()))
    tn = (((0,), (0,)), ((), ()))
    xb = x_ref[0].astype(BF16)
    for h in range(RET_HEADS):
        lg = LOG_GAMMA[h]
        q = q_ref[0, :, h * dk:(h + 1) * dk]
        k = k_ref[0, :, h * dk:(h + 1) * dk]
        v = v_ref[0, :, h * dv:(h + 1) * dv]
        scores = lax.dot_general(q, k, nt, preferred_element_type=F32) * decay_ref[h]
        q_in = (q.astype(F32) * q_decay_ref[h]).astype(BF16)
        k_out = (k.astype(F32) * k_decay_ref[h]).astype(BF16)
        state = state_ref[h]
        out = (jnp.dot(scores.astype(BF16), v, preferred_element_type=F32)
               + jnp.dot(q_in, state.astype(BF16), preferred_element_type=F32))
        state_ref[h] = math.exp(length * lg) * state + lax.dot_general(k_out, v, tn, preferred_element_type=F32)
        mu = jnp.mean(out, axis=-1, keepdims=True)
        oc = out - mu
        var = jnp.mean(oc * oc, axis=-1, keepdims=True)
        normed = oc * lax.rsqrt(var + GN_EPS)
        gate = jnp.dot(xb, wg_ref[:, h * dv:(h + 1) * dv], preferred_element_type=F32)
        o_ref[0, :, h * dv:(h + 1) * dv] = (jax.nn.silu(gate) * normed).astype(o_ref.dtype)


def _retention_core(qkv, x, w_in, *, layer, batch):
    t, d = x.shape
    v_dim = w_in.shape[2] - qkv.shape[1]
    qk_dim = (qkv.shape[1] - v_dim) // 2
    seq = t // batch
    length = min(RET_BLOCK, seq)
    assert seq % length == 0 and length % CHUNK == 0
    qkv3 = qkv.reshape(batch, seq, qkv.shape[1])
    v_block = 2 * qk_dim // v_dim
    gate_block = qkv.shape[1] // v_dim
    assert v_block * v_dim == 2 * qk_dim and gate_block * v_dim == qkv.shape[1]
    out = pl.pallas_call(
        _retention_kernel,
        grid=(batch, seq // length),
        in_specs=[
            pl.BlockSpec((1, length, qk_dim), lambda b, s: (b, s, 0)),
            pl.BlockSpec((1, length, qk_dim), lambda b, s: (b, s, 1)),
            pl.BlockSpec((1, length, v_dim), lambda b, s: (b, s, v_block)),
            pl.BlockSpec((1, length, d), lambda b, s: (b, s, 0)),
            pl.BlockSpec((None, d, v_dim), lambda b, s: (layer, 0, gate_block), pipeline_mode=pl.Buffered(1)),
        ],
        out_specs=pl.BlockSpec((1, length, v_dim), lambda b, s: (b, s, 0)),
        out_shape=jax.ShapeDtypeStruct((batch, seq, v_dim), BF16),
        scratch_shapes=[pltpu.VMEM((RET_HEADS, qk_dim // RET_HEADS, v_dim // RET_HEADS), F32),
                        pltpu.VMEM((RET_HEADS, length, length), F32),
                        pltpu.VMEM((RET_HEADS, length, qk_dim // RET_HEADS), F32),
                        pltpu.VMEM((RET_HEADS, length, qk_dim // RET_HEADS), F32)],
        compiler_params=_params("parallel", "arbitrary"),
        name="ret_core",
    )(qkv3, qkv3, qkv3, x.reshape(batch, seq, d), w_in)
    return out.reshape(t, v_dim)


def _pack_bf16_pairs(x):
    half = x.shape[1] // 2
    lo = lax.bitcast_convert_type(x[:, :half].astype(BF16).astype(F32), jnp.uint32)
    hi = lax.bitcast_convert_type(x[:, half:].astype(BF16).astype(F32), jnp.uint32)
    return hi | (lo >> 16)


def _unpack_bf16_pairs(xp):
    lo = lax.bitcast_convert_type(xp << 16, F32).astype(BF16)
    hi = lax.bitcast_convert_type(xp & jnp.uint32(0xFFFF0000), F32).astype(BF16)
    return jnp.concatenate([lo, hi], axis=1)


def _outproj_ln_router_kernel(o_ref, wo_ref, x_ref, gain_ref, bias_ref, wr_ref,
                              x1_ref, xp_ref, idx_ref, gates_ref, *, ln_row):
    w = wr_ref[...]
    w_hi = w.astype(BF16)
    w_lo = (w - w_hi.astype(F32)).astype(BF16)
    for r in range(0, x_ref.shape[0], SUB_ROWS):
        rows = pl.ds(r, SUB_ROWS)
        mix = jnp.dot(o_ref[rows, :], wo_ref[...], preferred_element_type=F32)
        x1 = _layer_norm(ALPHA * x_ref[rows, :] + mix, _row(gain_ref, ln_row), _row(bias_ref, ln_row))
        x1_ref[rows, :] = x1
        xp_ref[rows, :] = _pack_bf16_pairs(x1)
        x_hi = x1.astype(BF16)
        x_lo = (x1 - x_hi.astype(F32)).astype(BF16)
        logits = (jnp.dot(x_hi, w_hi, preferred_element_type=F32)
                  + (jnp.dot(x_hi, w_lo, preferred_element_type=F32)
                     + jnp.dot(x_lo, w_hi, preferred_element_type=F32)))
        expert = lax.broadcasted_iota(jnp.int32, logits.shape, 1).astype(F32)
        none = float(N_EXPERTS)
        m1 = jnp.max(logits, axis=1, keepdims=True)
        i1 = jnp.min(jnp.where(logits == m1, expert, none), axis=1, keepdims=True)
        rest = jnp.where(expert == i1, -jnp.inf, logits)
        m2 = jnp.max(rest, axis=1, keepdims=True)
        i2 = jnp.min(jnp.where(rest == m2, expert, none), axis=1, keepdims=True)
        p2 = jnp.exp(m2 - m1)
        denom = 1.0 + p2
        first = lax.broadcasted_iota(jnp.int32, (SUB_ROWS, TOP_K), 1) == 0
        idx_ref[rows, :] = jnp.where(first, i1, i2).astype(jnp.int32)
        gates_ref[rows, :] = jnp.where(first, 1.0 / denom, p2 / denom)


def _outproj_ln_router(o, w_o, x, gains, biases, w_router, *, layer, ln_row):
    t, d = x.shape
    v_dim = o.shape[1]
    tm = min(OUTPROJ_ROWS, t)
    assert t % tm == 0 and tm % SUB_ROWS == 0 and TOP_K == 2 and w_o.dtype == BF16
    full = lambda a: pl.BlockSpec(a.shape, lambda i: (0,) * a.ndim)
    return pl.pallas_call(
        functools.partial(_outproj_ln_router_kernel, ln_row=ln_row),
        grid=(t // tm,),
        in_specs=[
            pl.BlockSpec((tm, v_dim), lambda i: (i, 0)),
            pl.BlockSpec((None, v_dim, d), lambda i: (layer, 0, 0), pipeline_mode=pl.Buffered(1)),
            pl.BlockSpec((tm, d), lambda i: (i, 0)),
            full(gains), full(biases),
            pl.BlockSpec((None, d, N_EXPERTS), lambda i: (layer, 0, 0)),
        ],
        out_specs=[
            pl.BlockSpec((tm, d), lambda i: (i, 0)),
            pl.BlockSpec((tm, d // 2), lambda i: (i, 0)),
            pl.BlockSpec((tm, TOP_K), lambda i: (i, 0)),
            pl.BlockSpec((tm, TOP_K), lambda i: (i, 0)),
        ],
        out_shape=[
            jax.ShapeDtypeStruct((t, d), F32),
            jax.ShapeDtypeStruct((t, d // 2), jnp.uint32),
            jax.ShapeDtypeStruct((t, TOP_K), jnp.int32),
            jax.ShapeDtypeStruct((t, TOP_K), F32),
        ],
        compiler_params=_params("parallel"),
        name="outproj_ln_router",
    )(o, w_o, x, gains, biases, w_router)


def _route_plan(idx, block_rows):
    flat_e = idx.reshape(-1)
    n_assign = flat_e.shape[0]
    experts = jnp.arange(N_EXPERTS, dtype=jnp.int32)
    onehot = (flat_e[:, None] == experts[None, :]).astype(jnp.int32)
    running = jnp.cumsum(onehot, axis=0)
    rank = jnp.sum((running - onehot) * onehot, axis=1)
    counts = running[-1]
    padded = (counts + block_rows - 1) // block_rows * block_rows
    pad_end = jnp.cumsum(padded)
    pad_start = pad_end - padded
    dest = (jnp.sum(onehot * pad_start[None, :], axis=1) + rank).astype(jnp.int32)
    n_blocks = n_assign // block_rows + N_EXPERTS
    block_start = jnp.arange(n_blocks, dtype=jnp.int32) * block_rows
    block_expert = jnp.minimum(jnp.sum(block_start[:, None] >= pad_end[None, :], axis=1), N_EXPERTS - 1)
    live_rows = jnp.clip((pad_start + counts)[block_expert] - block_start, 0, block_rows)
    live_sub = (live_rows + SUB_ROWS - 1) // SUB_ROWS
    return dest, block_expert.astype(jnp.int32), live_sub.astype(jnp.int32), n_blocks


def _for_each_sublane_group(n_groups, body):
    lax.fori_loop(0, n_groups, lambda g, carry: (body(g), carry)[1], 0)


def _dispatch_kernel(dest_ref, x_ref, init_hbm, xs_hbm, sem):
    del init_hbm
    groups, sub = x_ref.shape[0], x_ref.shape[1]
    first = pl.program_id(0) * (groups * sub * TOP_K)

    def row_copies(g):
        at = first + g * (sub * TOP_K)
        return [pltpu.make_async_copy(x_ref.at[g, pl.ds(u, 1)], xs_hbm.at[pl.ds(dest_ref[at + TOP_K * u + k], 1)], sem)
                for u in range(sub) for k in range(TOP_K)]

    _for_each_sublane_group(groups, lambda g: [c.start() for c in row_copies(g)])
    _for_each_sublane_group(groups, lambda g: [c.wait() for c in row_copies(g)])


def _dispatch(x, dest, n_rows):
    t, d = x.shape
    tokens = min(ROUTE_ROWS, t)
    sub = V7X_F32_SUBLANES
    assert t % tokens == 0 and tokens % sub == 0
    return pl.pallas_call(
        _dispatch_kernel,
        grid_spec=pltpu.PrefetchScalarGridSpec(
            num_scalar_prefetch=1,
            grid=(t // tokens,),
            in_specs=[pl.BlockSpec((tokens // sub, sub, d), lambda i, dest: (i, 0, 0)),
                      pl.BlockSpec(memory_space=pl.ANY)],
            out_specs=pl.BlockSpec(memory_space=pl.ANY),
            scratch_shapes=[pltpu.SemaphoreType.DMA(())],
        ),
        out_shape=jax.ShapeDtypeStruct((n_rows, d), x.dtype),
        input_output_aliases={2: 0},
        compiler_params=pltpu.CompilerParams(dimension_semantics=("arbitrary",), has_side_effects=True),
        name="moe_dispatch",
    )(dest, x.reshape(t // sub, sub, d), jnp.zeros((n_rows, d), x.dtype))


def _expert_kernel(block_expert_ref, live_sub_ref, xs_ref, wg_ref, wu_ref, wd_ref, ys_ref,
                   xb_ref, wgb_ref, wub_ref, wdb_ref):
    del block_expert_ref
    i, f = pl.program_id(0), pl.program_id(1)
    live_sub = live_sub_ref[i]

    @pl.when(f == 0)
    def _():
        xb_ref[...] = _unpack_bf16_pairs(xs_ref[...])
        ys_ref[...] = jnp.zeros_like(ys_ref)

    for n_live in range(1, xs_ref.shape[0] // SUB_ROWS + 1):
        @pl.when(live_sub == n_live)
        def _():
            _cast_weights((wg_ref, wu_ref, wd_ref), (wgb_ref, wub_ref, wdb_ref))
            _swiglu_rows(pl.ds(0, n_live * SUB_ROWS), xb_ref, wgb_ref, wub_ref, wdb_ref, ys_ref)


def _experts(xs, block_expert, live_sub, w_gate, w_up, w_down, *, layer, block_rows):
    n_rows = xs.shape[0]
    d, d_ff = w_gate.shape[2], w_gate.shape[3]
    tf = min(MOE_COLS, d_ff)
    assert n_rows % block_rows == 0 and d_ff % tf == 0 and block_rows % SUB_ROWS == 0 and xs.shape[1] * 2 == d
    n_f = d_ff // tf

    def col(i, f, live):
        return jnp.where(live[i] > 0, f, n_f - 1)

    return pl.pallas_call(
        _expert_kernel,
        grid_spec=pltpu.PrefetchScalarGridSpec(
            num_scalar_prefetch=2,
            grid=(n_rows // block_rows, n_f),
            in_specs=[
                pl.BlockSpec((block_rows, d // 2), lambda i, f, be, live: (i, 0)),
                pl.BlockSpec((None, None, d, tf), lambda i, f, be, live: (layer, be[i], 0, col(i, f, live))),
                pl.BlockSpec((None, None, d, tf), lambda i, f, be, live: (layer, be[i], 0, col(i, f, live))),
                pl.BlockSpec((None, None, tf, d), lambda i, f, be, live: (layer, be[i], col(i, f, live), 0)),
            ],
            out_specs=pl.BlockSpec((block_rows, d), lambda i, f, be, live: (i, 0)),
            scratch_shapes=[pltpu.VMEM((block_rows, d), BF16), pltpu.VMEM((d, tf), BF16),
                            pltpu.VMEM((d, tf), BF16), pltpu.VMEM((tf, d), BF16)],
        ),
        out_shape=jax.ShapeDtypeStruct((n_rows, d), F32),
        compiler_params=_params("parallel", "arbitrary"),
        name="moe_experts",
    )(block_expert, live_sub, xs, w_gate, w_up, w_down)


def _combine_ln_kernel(dest_ref, x_ref, gates_ref, gain_ref, bias_ref, ys_hbm, o_ref, y_buf, sems, *, ln_row):
    groups, sub = x_ref.shape[0], x_ref.shape[1]
    tile = pl.program_id(0)
    per_tile = groups * sub * TOP_K

    def row_copies(which, g):
        slot = which & 1
        at = which * per_tile + g * (sub * TOP_K)
        return [pltpu.make_async_copy(ys_hbm.at[pl.ds(dest_ref[at + TOP_K * u + k], 1)],
                                      y_buf.at[slot, k, g, pl.ds(u, 1)], sems.at[slot])
                for u in range(sub) for k in range(TOP_K)]

    def gather(which):
        _for_each_sublane_group(groups, lambda g: [c.start() for c in row_copies(which, g)])

    @pl.when(tile == 0)
    def _():
        gather(tile)

    @pl.when(tile + 1 < pl.num_programs(0))
    def _():
        gather(tile + 1)

    _for_each_sublane_group(groups, lambda g: [c.wait() for c in row_copies(tile, g)])
    slot = tile & 1
    gates = gates_ref[...]
    mix = gates[:, :, 0:1] * y_buf[slot, 0] + gates[:, :, 1:2] * y_buf[slot, 1]
    o_ref[...] = _layer_norm(ALPHA * x_ref[...] + mix, _row(gain_ref, ln_row), _row(bias_ref, ln_row))


def _combine_ln(x, ys, dest, gates, gains, biases, *, ln_row):
    t, d = x.shape
    tokens = min(ROUTE_ROWS, t)
    sub = V7X_F32_SUBLANES
    assert t % tokens == 0 and tokens % sub == 0
    groups = tokens // sub
    full = lambda a: pl.BlockSpec(a.shape, lambda i, dest: (0,) * a.ndim)
    out = pl.pallas_call(
        functools.partial(_combine_ln_kernel, ln_row=ln_row),
        grid_spec=pltpu.PrefetchScalarGridSpec(
            num_scalar_prefetch=1,
            grid=(t // tokens,),
            in_specs=[
                pl.BlockSpec((groups, sub, d), lambda i, dest: (i, 0, 0)),
                pl.BlockSpec((groups, sub, TOP_K), lambda i, dest: (i, 0, 0)),
                full(gains), full(biases),
                pl.BlockSpec(memory_space=pl.ANY),
            ],
            out_specs=pl.BlockSpec((groups, sub, d), lambda i, dest: (i, 0, 0)),
            scratch_shapes=[pltpu.VMEM((2, TOP_K, groups, sub, d), F32), pltpu.SemaphoreType.DMA((2,))],
        ),
        out_shape=jax.ShapeDtypeStruct((t // sub, sub, d), F32),
        compiler_params=_params("arbitrary"),
        name="moe_combine_ln",
    )(dest, x.reshape(t // sub, sub, d), gates.reshape(t // sub, sub, TOP_K), gains, biases, ys)
    return out.reshape(t, d)


def _moe_ln(x, x_packed, idx, gates, w_gate, w_up, w_down, gains, biases, *, layer, ln_row):
    dest, block_expert, live_sub, n_blocks = _route_plan(idx, MOE_ROWS)
    xs = _dispatch(x_packed, dest, n_blocks * MOE_ROWS)
    ys = _experts(xs, block_expert, live_sub, w_gate, w_up, w_down, layer=layer, block_rows=MOE_ROWS)
    return _combine_ln(x, ys, dest, gates, gains, biases, ln_row=ln_row)


def _rope_tables(seq, head_dim):
    half = head_dim // 2
    inv_freq = ROPE_BASE ** (-jnp.arange(half, dtype=F32) / half)
    ang = jnp.arange(seq, dtype=F32)[:, None] * inv_freq[None, :]
    return jnp.cos(ang), jnp.sin(ang)


def kernel(x, ln_gain, ln_bias, pool_w, pool_scale, ret_w_in, ret_w_o, ffn_w_gate, ffn_w_up, ffn_w_down,
           moe_w_router, moe_w_gate, moe_w_up, moe_w_down):
    batch, seq, d = x.shape
    t = batch * seq
    gains = ln_gain.reshape(2 * DEPTH, d)
    biases = ln_bias.reshape(2 * DEPTH, d)
    qk_dim = ret_w_o.shape[2]
    v_dim = ret_w_o.shape[1]
    head_dim = qk_dim // RET_HEADS
    cos, sin = _rope_tables(seq, head_dim)
    pool_w = pool_w.astype(BF16)
    ret_w_in = ret_w_in.astype(BF16)
    ret_w_o = ret_w_o.astype(BF16)
    ffn_w_gate, ffn_w_up, ffn_w_down = (w.astype(BF16) for w in (ffn_w_gate, ffn_w_up, ffn_w_down))

    for i in range(DEPTH):
        j = i // 2
        if i % 2 == 0:
            x = _pool_ln(x, pool_w, pool_scale, gains, biases, layer=j, ln_row=2 * i)
            x2 = _swiglu_ln(x.reshape(t, d), ffn_w_gate, ffn_w_up, ffn_w_down, gains, biases,
                            layer=j, ln_row=2 * i + 1)
        else:
            x2 = x.reshape(t, d)
            qkv = _project(x2, ret_w_in, cos, sin, layer=j, col_start=0, n_cols=2 * qk_dim + v_dim,
                           rope_cols=2 * qk_dim, head_dim=head_dim, out_dtype=BF16)
            o = _retention_core(qkv, x2, ret_w_in, layer=j, batch=batch)
            x1, x1_packed, idx, gates = _outproj_ln_router(o, ret_w_o, x2, gains, biases, moe_w_router,
                                                           layer=j, ln_row=2 * i)
            x2 = _moe_ln(x1, x1_packed, idx, gates, moe_w_gate, moe_w_up, moe_w_down, gains, biases,
                         layer=j, ln_row=2 * i + 1)
        x = x2.reshape(batch, seq, d)
    return x
```

```python
import functools
import math

import jax
import jax.numpy as jnp
from jax import lax
from jax.experimental import pallas as pl
from jax.experimental.pallas import tpu as pltpu

F32 = jnp.float32
BF16 = jnp.bfloat16

DEPTH = 4
POOL_WINDOWS = (2, 4, 8, 16)
RET_HEADS = 8
CHUNK = 64
ROPE_BASE = 10000.0
N_EXPERTS = 8
TOP_K = 2
LN_EPS = 1e-5
GN_EPS = 1e-6
ALPHA = (2 * DEPTH) ** 0.25
LOG_GAMMA = tuple(math.log1p(-(2.0 ** (-5 - h))) for h in range(RET_HEADS))

V7X_VMEM_BYTES = 64 * 1024 * 1024
V7X_F32_SUBLANES = 8
POOL_HALO = 2 * V7X_F32_SUBLANES

VMEM_LIMIT = V7X_VMEM_BYTES - 8 * 1024 * 1024
POOL_ROWS = 512
PROJ_ROWS, PROJ_COLS = 1024, 1024
RET_BLOCK = 256
OUTPROJ_ROWS = 512
FFN_ROWS, FFN_COLS = 512, 512
MOE_ROWS, MOE_COLS = 1024, 256
SUB_ROWS = 256
ROUTE_ROWS = 512
COMBINE_GROUPS = 16


def _params(*semantics):
    return pltpu.CompilerParams(dimension_semantics=semantics, vmem_limit_bytes=VMEM_LIMIT)


def _layer_norm(z, gain, bias):
    mu = jnp.mean(z, axis=-1, keepdims=True)
    zc = z - mu
    var = jnp.mean(zc * zc, axis=-1, keepdims=True)
    return zc * lax.rsqrt(var + LN_EPS) * gain + bias


def _row(ref, r):
    return ref[r:r + 1, :]


def _pool_ln_kernel(x_ref, halo_ref, w_ref, scale_ref, gain_ref, bias_ref, o_ref, *, layer, ln_row):
    ts, d = x_ref.shape[1], x_ref.shape[2]
    group = d // len(POOL_WINDOWS)
    s = pl.program_id(1)
    x = x_ref[0]
    halo = jnp.where(s > 0, halo_ref[0], 0.0)
    ext = jnp.concatenate([halo, x], axis=0)
    seen = (s * ts + 1 + lax.broadcasted_iota(jnp.int32, (ts, 1), 0)).astype(F32)
    mixed = []
    for g, win in enumerate(POOL_WINDOWS):
        cols = slice(g * group, (g + 1) * group)
        acc = ext[:, cols]
        span = 1
        while span < win:
            acc = acc + pltpu.roll(acc, span, 0)
            span *= 2
        pooled = acc[POOL_HALO:] * (1.0 / jnp.minimum(seen, float(win)))
        diff = (pooled - x[:, cols]).astype(BF16)
        mixed.append(jnp.dot(diff, w_ref[g].astype(BF16), preferred_element_type=F32))
    mix = jnp.concatenate(mixed, axis=-1) * _row(scale_ref, layer)
    o_ref[0] = _layer_norm(ALPHA * x + mix, _row(gain_ref, ln_row), _row(bias_ref, ln_row))


def _pool_ln(x, pool_w, pool_scale, gains, biases, *, layer, ln_row):
    b, s, d = x.shape
    ts = min(POOL_ROWS, s)
    assert s % ts == 0 and ts % POOL_HALO == 0 and max(POOL_WINDOWS) <= POOL_HALO
    halo_blocks = ts // POOL_HALO
    n_groups, group = pool_w.shape[1], pool_w.shape[2]
    full = lambda a: pl.BlockSpec(a.shape, lambda bi, si: (0,) * a.ndim)
    return pl.pallas_call(
        functools.partial(_pool_ln_kernel, layer=layer, ln_row=ln_row),
        grid=(b, s // ts),
        in_specs=[
            pl.BlockSpec((1, ts, d), lambda bi, si: (bi, si, 0)),
            pl.BlockSpec((1, POOL_HALO, d), lambda bi, si: (bi, jnp.maximum(si * halo_blocks - 1, 0), 0)),
            pl.BlockSpec((None, n_groups, group, group), lambda bi, si: (layer, 0, 0, 0)),
            full(pool_scale), full(gains), full(biases),
        ],
        out_specs=pl.BlockSpec((1, ts, d), lambda bi, si: (bi, si, 0)),
        out_shape=jax.ShapeDtypeStruct(x.shape, F32),
        compiler_params=_params("parallel", "parallel"),
        name="pool_ln",
    )(x, x, pool_w, pool_scale, gains, biases)


def _cast_weights(w_refs, wb_refs):
    for w_ref, wb_ref in zip(w_refs, wb_refs):
        wb_ref[...] = w_ref[...].astype(BF16)


def _swiglu_rows(rows, xb_ref, wgb_ref, wub_ref, wdb_ref, acc_ref):
    xb = xb_ref[rows, :]
    gate = jnp.dot(xb, wgb_ref[...], preferred_element_type=F32)
    up = jnp.dot(xb, wub_ref[...], preferred_element_type=F32)
    hidden = (jax.nn.silu(gate) * up).astype(BF16)
    acc_ref[rows, :] += jnp.dot(hidden, wdb_ref[...], preferred_element_type=F32)


def _swiglu_ln_kernel(x_ref, wg_ref, wu_ref, wd_ref, gain_ref, bias_ref, o_ref, xb_ref, *, ln_row):
    f = pl.program_id(1)
    sub_blocks = [pl.ds(r, SUB_ROWS) for r in range(0, x_ref.shape[0], SUB_ROWS)]

    @pl.when(f == 0)
    def _():
        xb_ref[...] = x_ref[...].astype(BF16)
        o_ref[...] = jnp.zeros_like(o_ref)

    _swiglu_rows(pl.ds(0, x_ref.shape[0]), xb_ref, wg_ref, wu_ref, wd_ref, o_ref)

    @pl.when(f == pl.num_programs(1) - 1)
    def _():
        for rows in sub_blocks:
            o_ref[rows, :] = _layer_norm(ALPHA * x_ref[rows, :] + o_ref[rows, :],
                                         _row(gain_ref, ln_row), _row(bias_ref, ln_row))


def _swiglu_ln(x, w_gate, w_up, w_down, gains, biases, *, layer, ln_row):
    t, d = x.shape
    d_ff = w_gate.shape[2]
    tm, tf = min(FFN_ROWS, t), min(FFN_COLS, d_ff)
    assert t % tm == 0 and d_ff % tf == 0 and tm % SUB_ROWS == 0
    full = lambda a: pl.BlockSpec(a.shape, lambda i, f: (0,) * a.ndim)
    return pl.pallas_call(
        functools.partial(_swiglu_ln_kernel, ln_row=ln_row),
        grid=(t // tm, d_ff // tf),
        in_specs=[
            pl.BlockSpec((tm, d), lambda i, f: (i, 0)),
            pl.BlockSpec((None, d, tf), lambda i, f: (layer, 0, f)),
            pl.BlockSpec((None, d, tf), lambda i, f: (layer, 0, f)),
            pl.BlockSpec((None, tf, d), lambda i, f: (layer, f, 0)),
            full(gains), full(biases),
        ],
        out_specs=pl.BlockSpec((tm, d), lambda i, f: (i, 0)),
        out_shape=jax.ShapeDtypeStruct((t, d), F32),
        scratch_shapes=[pltpu.VMEM((tm, d), BF16)],
        compiler_params=_params("parallel", "arbitrary"),
        name="swiglu_ln",
    )(x, w_gate, w_up, w_down, gains, biases)


def _proj_kernel(x_ref, w_ref, cos_ref, sin_ref, o_ref, xb_ref, *, rope_cols, k_cols_from, head_dim):
    j = pl.program_id(1)
    tn = o_ref.shape[1]

    @pl.when(j == 0)
    def _():
        xb_ref[...] = x_ref[...].astype(BF16)

    y = jnp.dot(xb_ref[...], w_ref[...].astype(BF16), preferred_element_type=F32)

    @pl.when(j * tn < rope_cols)
    def _():
        half = head_dim // 2
        cos, sin = cos_ref[...], sin_ref[...]
        scale = jnp.where(j * tn >= k_cols_from, head_dim ** -0.5, 1.0).astype(F32)
        parts = []
        for h in range(tn // head_dim):
            a1 = y[:, h * head_dim:h * head_dim + half]
            a2 = y[:, h * head_dim + half:(h + 1) * head_dim]
            parts.append((a1 * cos - a2 * sin) * scale)
            parts.append((a1 * sin + a2 * cos) * scale)
        o_ref[...] = jnp.concatenate(parts, axis=-1).astype(o_ref.dtype)

    @pl.when(j * tn >= rope_cols)
    def _():
        o_ref[...] = y.astype(o_ref.dtype)


def _project(x, w_in, cos, sin, *, layer, col_start, n_cols, rope_cols, head_dim, out_dtype):
    t, d = x.shape
    seq = cos.shape[0]
    tm, tn = min(PROJ_ROWS, seq), PROJ_COLS
    assert t % tm == 0 and seq % tm == 0 and n_cols % tn == 0 and col_start % tn == 0
    assert tn % head_dim == 0 and rope_cols % tn == 0 and cos.shape[1] == head_dim // 2
    col0 = col_start // tn
    seq_tiles = seq // tm
    return pl.pallas_call(
        functools.partial(_proj_kernel, rope_cols=rope_cols, k_cols_from=rope_cols // 2, head_dim=head_dim),
        grid=(t // tm, n_cols // tn),
        in_specs=[
            pl.BlockSpec((tm, d), lambda i, j: (i, 0)),
            pl.BlockSpec((None, d, tn), lambda i, j: (layer, 0, col0 + j)),
            pl.BlockSpec((tm, head_dim // 2), lambda i, j: (i % seq_tiles, 0)),
            pl.BlockSpec((tm, head_dim // 2), lambda i, j: (i % seq_tiles, 0)),
        ],
        out_specs=pl.BlockSpec((tm, tn), lambda i, j: (i, j)),
        out_shape=jax.ShapeDtypeStruct((t, n_cols), out_dtype),
        scratch_shapes=[pltpu.VMEM((tm, d), BF16)],
        compiler_params=_params("parallel", "arbitrary"),
        name="ret_proj",
    )(x, w_in, cos, sin)


def _retention_kernel(q_ref, k_ref, v_ref, x_ref, wg_ref, o_ref, state_ref, decay_ref, q_decay_ref, k_decay_ref):
    blk = pl.program_id(1)
    length = q_ref.shape[1]
    dk = q_ref.shape[2] // RET_HEADS
    dv = v_ref.shape[2] // RET_HEADS

    @pl.when(blk == 0)
    def _():
        state_ref[...] = jnp.zeros_like(state_ref)
        row = lax.broadcasted_iota(jnp.int32, (length, length), 0)
        col = lax.broadcasted_iota(jnp.int32, (length, length), 1)
        dist = jnp.abs(row - col).astype(F32)
        visible = (col // CHUNK) <= (row // CHUNK)
        pos = lax.broadcasted_iota(jnp.int32, (length, dk), 0).astype(F32)
        for h in range(RET_HEADS):
            lg = LOG_GAMMA[h]
            decay_ref[h] = jnp.where(visible, jnp.exp(dist * lg), 0.0)
            q_decay_ref[h] = jnp.exp((pos + 1.0) * lg)
            k_decay_ref[h] = jnp.exp((length - 1.0 - pos) * lg)

    nt = (((1,), (1,)), ((), ()))
    tn = (((0,), (0,)), ((), ()))
    xb = x_ref[0].astype(BF16)
    for h in range(RET_HEADS):
        lg = LOG_GAMMA[h]
        q = q_ref[0, :, h * dk:(h + 1) * dk]
        k = k_ref[0, :, h * dk:(h + 1) * dk]
        v = v_ref[0, :, h * dv:(h + 1) * dv]
        scores = lax.dot_general(q, k, nt, preferred_element_type=F32) * decay_ref[h]
        q_in = (q.astype(F32) * q_decay_ref[h]).astype(BF16)
        k_out = (k.astype(F32) * k_decay_ref[h]).astype(BF16)
        state = state_ref[h]
        out = (jnp.dot(scores.astype(BF16), v, preferred_element_type=F32)
               + jnp.dot(q_in, state.astype(BF16), preferred_element_type=F32))
        state_ref[h] = math.exp(length * lg) * state + lax.dot_general(k_out, v, tn, preferred_element_type=F32)
        mu = jnp.mean(out, axis=-1, keepdims=True)
        oc = out - mu
        var = jnp.mean(oc * oc, axis=-1, keepdims=True)
        normed = oc * lax.rsqrt(var + GN_EPS)
        gate = jnp.dot(xb, wg_ref[:, h * dv:(h + 1) * dv], preferred_element_type=F32)
        o_ref[0, :, h * dv:(h + 1) * dv] = (jax.nn.silu(gate) * normed).astype(o_ref.dtype)


def _retention_core(qkv, x, w_in, *, layer, batch):
    t, d = x.shape
    v_dim = w_in.shape[2] - qkv.shape[1]
    qk_dim = (qkv.shape[1] - v_dim) // 2
    seq = t // batch
    length = min(RET_BLOCK, seq)
    assert seq % length == 0 and length % CHUNK == 0
    qkv3 = qkv.reshape(batch, seq, qkv.shape[1])
    v_block = 2 * qk_dim // v_dim
    gate_block = qkv.shape[1] // v_dim
    assert v_block * v_dim == 2 * qk_dim and gate_block * v_dim == qkv.shape[1]
    out = pl.pallas_call(
        _retention_kernel,
        grid=(batch, seq // length),
        in_specs=[
            pl.BlockSpec((1, length, qk_dim), lambda b, s: (b, s, 0)),
            pl.BlockSpec((1, length, qk_dim), lambda b, s: (b, s, 1)),
            pl.BlockSpec((1, length, v_dim), lambda b, s: (b, s, v_block)),
            pl.BlockSpec((1, length, d), lambda b, s: (b, s, 0)),
            pl.BlockSpec((None, d, v_dim), lambda b, s: (layer, 0, gate_block), pipeline_mode=pl.Buffered(1)),
        ],
        out_specs=pl.BlockSpec((1, length, v_dim), lambda b, s: (b, s, 0)),
        out_shape=jax.ShapeDtypeStruct((batch, seq, v_dim), BF16),
        scratch_shapes=[pltpu.VMEM((RET_HEADS, qk_dim // RET_HEADS, v_dim // RET_HEADS), F32),
                        pltpu.VMEM((RET_HEADS, length, length), F32),
                        pltpu.VMEM((RET_HEADS, length, qk_dim // RET_HEADS), F32),
                        pltpu.VMEM((RET_HEADS, length, qk_dim // RET_HEADS), F32)],
        compiler_params=_params("parallel", "arbitrary"),
        name="ret_core",
    )(qkv3, qkv3, qkv3, x.reshape(batch, seq, d), w_in)
    return out.reshape(t, v_dim)


def _pack_bf16_pairs(x):
    half = x.shape[1] // 2
    lo = lax.bitcast_convert_type(x[:, :half].astype(BF16).astype(F32), jnp.uint32)
    hi = lax.bitcast_convert_type(x[:, half:].astype(BF16).astype(F32), jnp.uint32)
    return hi | (lo >> 16)


def _unpack_bf16_pairs(xp):
    lo = lax.bitcast_convert_type(xp << 16, F32).astype(BF16)
    hi = lax.bitcast_convert_type(xp & jnp.uint32(0xFFFF0000), F32).astype(BF16)
    return jnp.concatenate([lo, hi], axis=1)


def _outproj_ln_router_kernel(o_ref, wo_ref, x_ref, gain_ref, bias_ref, wr_ref,
                              x1_ref, xp_ref, idx_ref, gates_ref, *, ln_row):
    w = wr_ref[...]
    w_hi = w.astype(BF16)
    w_lo = (w - w_hi.astype(F32)).astype(BF16)
    for r in range(0, x_ref.shape[0], SUB_ROWS):
        rows = pl.ds(r, SUB_ROWS)
        mix = jnp.dot(o_ref[rows, :], wo_ref[...], preferred_element_type=F32)
        x1 = _layer_norm(ALPHA * x_ref[rows, :] + mix, _row(gain_ref, ln_row), _row(bias_ref, ln_row))
        x1_ref[rows, :] = x1
        xp_ref[rows, :] = _pack_bf16_pairs(x1)
        x_hi = x1.astype(BF16)
        x_lo = (x1 - x_hi.astype(F32)).astype(BF16)
        logits = (jnp.dot(x_hi, w_hi, preferred_element_type=F32)
                  + (jnp.dot(x_hi, w_lo, preferred_element_type=F32)
                     + jnp.dot(x_lo, w_hi, preferred_element_type=F32)))
        expert = lax.broadcasted_iota(jnp.int32, logits.shape, 1).astype(F32)
        none = float(N_EXPERTS)
        m1 = jnp.max(logits, axis=1, keepdims=True)
        i1 = jnp.min(jnp.where(logits == m1, expert, none), axis=1, keepdims=True)
        rest = jnp.where(expert == i1, -jnp.inf, logits)
        m2 = jnp.max(rest, axis=1, keepdims=True)
        i2 = jnp.min(jnp.where(rest == m2, expert, none), axis=1, keepdims=True)
        p2 = jnp.exp(m2 - m1)
        denom = 1.0 + p2
        first = lax.broadcasted_iota(jnp.int32, (SUB_ROWS, TOP_K), 1) == 0
        idx_ref[rows, :] = jnp.where(first, i1, i2).astype(jnp.int32)
        gates_ref[rows, :] = jnp.where(first, 1.0 / denom, p2 / denom)


def _outproj_ln_router(o, w_o, x, gains, biases, w_router, *, layer, ln_row):
    t, d = x.shape
    v_dim = o.shape[1]
    tm = min(OUTPROJ_ROWS, t)
    assert t % tm == 0 and tm % SUB_ROWS == 0 and TOP_K == 2 and w_o.dtype == BF16
    full = lambda a: pl.BlockSpec(a.shape, lambda i: (0,) * a.ndim)
    return pl.pallas_call(
        functools.partial(_outproj_ln_router_kernel, ln_row=ln_row),
        grid=(t // tm,),
        in_specs=[
            pl.BlockSpec((tm, v_dim), lambda i: (i, 0)),
            pl.BlockSpec((None, v_dim, d), lambda i: (layer, 0, 0), pipeline_mode=pl.Buffered(1)),
            pl.BlockSpec((tm, d), lambda i: (i, 0)),
            full(gains), full(biases),
            pl.BlockSpec((None, d, N_EXPERTS), lambda i: (layer, 0, 0)),
        ],
        out_specs=[
            pl.BlockSpec((tm, d), lambda i: (i, 0)),
            pl.BlockSpec((tm, d // 2), lambda i: (i, 0)),
            pl.BlockSpec((tm, TOP_K), lambda i: (i, 0)),
            pl.BlockSpec((tm, TOP_K), lambda i: (i, 0)),
        ],
        out_shape=[
            jax.ShapeDtypeStruct((t, d), F32),
            jax.ShapeDtypeStruct((t, d // 2), jnp.uint32),
            jax.ShapeDtypeStruct((t, TOP_K), jnp.int32),
            jax.ShapeDtypeStruct((t, TOP_K), F32),
        ],
        compiler_params=_params("parallel"),
        name="outproj_ln_router",
    )(o, w_o, x, gains, biases, w_router)


def _route_plan(idx, block_rows):
    flat_e = idx.reshape(-1)
    n_assign = flat_e.shape[0]
    experts = jnp.arange(N_EXPERTS, dtype=jnp.int32)
    onehot = (flat_e[:, None] == experts[None, :]).astype(jnp.int32)
    running = jnp.cumsum(onehot, axis=0)
    rank = jnp.sum((running - onehot) * onehot, axis=1)
    counts = running[-1]
    padded = (counts + block_rows - 1) // block_rows * block_rows
    pad_end = jnp.cumsum(padded)
    pad_start = pad_end - padded
    dest = (jnp.sum(onehot * pad_start[None, :], axis=1) + rank).astype(jnp.int32)
    n_blocks = n_assign // block_rows + N_EXPERTS
    block_start = jnp.arange(n_blocks, dtype=jnp.int32) * block_rows
    block_expert = jnp.minimum(jnp.sum(block_start[:, None] >= pad_end[None, :], axis=1), N_EXPERTS - 1)
    live_rows = jnp.clip((pad_start + counts)[block_expert] - block_start, 0, block_rows)
    live_sub = (live_rows + SUB_ROWS - 1) // SUB_ROWS
    return dest, block_expert.astype(jnp.int32), live_sub.astype(jnp.int32), n_blocks


def _for_each_sublane_group(n_groups, body):
    lax.fori_loop(0, n_groups, lambda g, carry: (body(g), carry)[1], 0)


def _dispatch_kernel(dest_ref, x_ref, init_hbm, xs_hbm, sem):
    del init_hbm
    groups, sub = x_ref.shape[0], x_ref.shape[1]
    first = pl.program_id(0) * (groups * sub * TOP_K)

    def row_copies(g):
        at = first + g * (sub * TOP_K)
        return [pltpu.make_async_copy(x_ref.at[g, pl.ds(u, 1)], xs_hbm.at[pl.ds(dest_ref[at + TOP_K * u + k], 1)], sem)
                for u in range(sub) for k in range(TOP_K)]

    _for_each_sublane_group(groups, lambda g: [c.start() for c in row_copies(g)])
    _for_each_sublane_group(groups, lambda g: [c.wait() for c in row_copies(g)])


def _dispatch(x, dest, n_rows):
    t, d = x.shape
    tokens = min(ROUTE_ROWS, t)
    sub = V7X_F32_SUBLANES
    assert t % tokens == 0 and tokens % sub == 0
    return pl.pallas_call(
        _dispatch_kernel,
        grid_spec=pltpu.PrefetchScalarGridSpec(
            num_scalar_prefetch=1,
            grid=(t // tokens,),
            in_specs=[pl.BlockSpec((tokens // sub, sub, d), lambda i, dest: (i, 0, 0)),
                      pl.BlockSpec(memory_space=pl.ANY)],
            out_specs=pl.BlockSpec(memory_space=pl.ANY),
            scratch_shapes=[pltpu.SemaphoreType.DMA(())],
        ),
        out_shape=jax.ShapeDtypeStruct((n_rows, d), x.dtype),
        input_output_aliases={2: 0},
        compiler_params=pltpu.CompilerParams(dimension_semantics=("arbitrary",), has_side_effects=True),
        name="moe_dispatch",
    )(dest, x.reshape(t // sub, sub, d), jnp.zeros((n_rows, d), x.dtype))


def _expert_kernel(block_expert_ref, live_sub_ref, xs_ref, wg_ref, wu_ref, wd_ref, ys_ref,
                   xb_ref, wgb_ref, wub_ref, wdb_ref):
    del block_expert_ref
    i, f = pl.program_id(0), pl.program_id(1)
    live_sub = live_sub_ref[i]

    @pl.when(f == 0)
    def _():
        xb_ref[...] = _unpack_bf16_pairs(xs_ref[...])
        ys_ref[...] = jnp.zeros_like(ys_ref)

    n_sub = xs_ref.shape[0] // SUB_ROWS
    sub_blocks = [pl.ds(sb * SUB_ROWS, SUB_ROWS) for sb in range(n_sub)]

    @pl.when(live_sub == n_sub)
    def _():
        _cast_weights((wg_ref, wu_ref, wd_ref), (wgb_ref, wub_ref, wdb_ref))
        _swiglu_rows(pl.ds(0, xs_ref.shape[0]), xb_ref, wgb_ref, wub_ref, wdb_ref, ys_ref)

    @pl.when(jnp.logical_and(live_sub > 0, live_sub < n_sub))
    def _():
        _cast_weights((wg_ref, wu_ref, wd_ref), (wgb_ref, wub_ref, wdb_ref))
        for sb in range(n_sub - 1):
            @pl.when(sb < live_sub)
            def _():
                _swiglu_rows(sub_blocks[sb], xb_ref, wgb_ref, wub_ref, wdb_ref, ys_ref)


def _experts(xs, block_expert, live_sub, w_gate, w_up, w_down, *, layer, block_rows):
    n_rows = xs.shape[0]
    d, d_ff = w_gate.shape[2], w_gate.shape[3]
    tf = min(MOE_COLS, d_ff)
    assert n_rows % block_rows == 0 and d_ff % tf == 0 and block_rows % SUB_ROWS == 0 and xs.shape[1] * 2 == d
    n_f = d_ff // tf

    def col(i, f, live):
        return jnp.where(live[i] > 0, f, n_f - 1)

    return pl.pallas_call(
        _expert_kernel,
        grid_spec=pltpu.PrefetchScalarGridSpec(
            num_scalar_prefetch=2,
            grid=(n_rows // block_rows, n_f),
            in_specs=[
                pl.BlockSpec((block_rows, d // 2), lambda i, f, be, live: (i, 0)),
                pl.BlockSpec((None, None, d, tf), lambda i, f, be, live: (layer, be[i], 0, col(i, f, live))),
                pl.BlockSpec((None, None, d, tf), lambda i, f, be, live: (layer, be[i], 0, col(i, f, live))),
                pl.BlockSpec((None, None, tf, d), lambda i, f, be, live: (layer, be[i], col(i, f, live), 0)),
            ],
            out_specs=pl.BlockSpec((block_rows, d), lambda i, f, be, live: (i, 0)),
            scratch_shapes=[pltpu.VMEM((block_rows, d), BF16), pltpu.VMEM((d, tf), BF16),
                            pltpu.VMEM((d, tf), BF16), pltpu.VMEM((tf, d), BF16)],
        ),
        out_shape=jax.ShapeDtypeStruct((n_rows, d), F32),
        compiler_params=_params("parallel", "arbitrary"),
        name="moe_experts",
    )(block_expert, live_sub, xs, w_gate, w_up, w_down)


def _combine_ln_kernel(dest_ref, x_ref, gates_ref, gain_ref, bias_ref, ys_hbm, o_ref, y_even, y_odd, sems, *, ln_row):
    groups, sub = x_ref.shape[0], x_ref.shape[1]
    tile = pl.program_id(0)
    per_tile = groups * sub * TOP_K
    bufs = (y_even, y_odd)

    def row_copies(which, parity, g):
        at = which * per_tile + g * (sub * TOP_K)
        return [pltpu.make_async_copy(ys_hbm.at[pl.ds(dest_ref[at + TOP_K * u + k], 1)],
                                      bufs[parity].at[k, g, pl.ds(u, 1)], sems.at[parity])
                for u in range(sub) for k in range(TOP_K)]

    def normalise(parity, g0):
        rows = pl.ds(g0, COMBINE_GROUPS)
        gates = gates_ref[rows]
        mix = gates[:, :, 0:1] * bufs[parity][0, rows] + gates[:, :, 1:2] * bufs[parity][1, rows]
        o_ref[rows] = _layer_norm(ALPHA * x_ref[rows] + mix, _row(gain_ref, ln_row), _row(bias_ref, ln_row))

    @pl.when(tile == 0)
    def _():
        _for_each_sublane_group(groups, lambda g: [c.start() for c in row_copies(tile, 0, g)])

    for parity in range(2):
        @pl.when((tile & 1) == parity)
        def _():
            _for_each_sublane_group(groups, lambda g: [c.wait() for c in row_copies(tile, parity, g)])

            @pl.when(tile + 1 < pl.num_programs(0))
            def _():
                def step(i):
                    g0 = pl.multiple_of(i * COMBINE_GROUPS, COMBINE_GROUPS)
                    for g in range(COMBINE_GROUPS):
                        for c in row_copies(tile + 1, 1 - parity, g0 + g):
                            c.start()
                    normalise(parity, g0)
                _for_each_sublane_group(groups // COMBINE_GROUPS, step)

            @pl.when(tile + 1 == pl.num_programs(0))
            def _():
                _for_each_sublane_group(
                    groups // COMBINE_GROUPS,
                    lambda i: normalise(parity, pl.multiple_of(i * COMBINE_GROUPS, COMBINE_GROUPS)))


def _combine_ln(x, ys, dest, gates, gains, biases, *, ln_row):
    t, d = x.shape
    tokens = min(ROUTE_ROWS, t)
    sub = V7X_F32_SUBLANES
    assert t % tokens == 0 and tokens % (sub * COMBINE_GROUPS) == 0
    groups = tokens // sub
    full = lambda a: pl.BlockSpec(a.shape, lambda i, dest: (0,) * a.ndim)
    out = pl.pallas_call(
        functools.partial(_combine_ln_kernel, ln_row=ln_row),
        grid_spec=pltpu.PrefetchScalarGridSpec(
            num_scalar_prefetch=1,
            grid=(t // tokens,),
            in_specs=[
                pl.BlockSpec((groups, sub, d), lambda i, dest: (i, 0, 0)),
                pl.BlockSpec((groups, sub, TOP_K), lambda i, dest: (i, 0, 0)),
                full(gains), full(biases),
                pl.BlockSpec(memory_space=pl.ANY),
            ],
            out_specs=pl.BlockSpec((groups, sub, d), lambda i, dest: (i, 0, 0)),
            scratch_shapes=[pltpu.VMEM((TOP_K, groups, sub, d), F32), pltpu.VMEM((TOP_K, groups, sub, d), F32),
                            pltpu.SemaphoreType.DMA((2,))],
        ),
        out_shape=jax.ShapeDtypeStruct((t // sub, sub, d), F32),
        compiler_params=_params("arbitrary"),
        name="moe_combine_ln",
    )(dest, x.reshape(t // sub, sub, d), gates.reshape(t // sub, sub, TOP_K), gains, biases, ys)
    return out.reshape(t, d)


def _moe_ln(x, x_packed, idx, gates, w_gate, w_up, w_down, gains, biases, *, layer, ln_row):
    dest, block_expert, live_sub, n_blocks = _route_plan(idx, MOE_ROWS)
    xs = _dispatch(x_packed, dest, n_blocks * MOE_ROWS)
    ys = _experts(xs, block_expert, live_sub, w_gate, w_up, w_down, layer=layer, block_rows=MOE_ROWS)
    return _combine_ln(x, ys, dest, gates, gains, biases, ln_row=ln_row)


def _rope_tables(seq, head_dim):
    half = head_dim // 2
    inv_freq = ROPE_BASE ** (-jnp.arange(half, dtype=F32) / half)
    ang = jnp.arange(seq, dtype=F32)[:, None] * inv_freq[None, :]
    return jnp.cos(ang), jnp.sin(ang)


def kernel(x, ln_gain, ln_bias, pool_w, pool_scale, ret_w_in, ret_w_o, ffn_w_gate, ffn_w_up, ffn_w_down,
           moe_w_router, moe_w_gate, moe_w_up, moe_w_down):
    batch, seq, d = x.shape
    t = batch * seq
    gains = ln_gain.reshape(2 * DEPTH, d)
    biases = ln_bias.reshape(2 * DEPTH, d)
    qk_dim = ret_w_o.shape[2]
    v_dim = ret_w_o.shape[1]
    head_dim = qk_dim // RET_HEADS
    cos, sin = _rope_tables(seq, head_dim)
    pool_w = pool_w.astype(BF16)
    ret_w_in = ret_w_in.astype(BF16)
    ret_w_o = ret_w_o.astype(BF16)
    ffn_w_gate, ffn_w_up, ffn_w_down = (w.astype(BF16) for w in (ffn_w_gate, ffn_w_up, ffn_w_down))

    for i in range(DEPTH):
        j = i // 2
        if i % 2 == 0:
            x = _pool_ln(x, pool_w, pool_scale, gains, biases, layer=j, ln_row=2 * i)
            x2 = _swiglu_ln(x.reshape(t, d), ffn_w_gate, ffn_w_up, ffn_w_down, gains, biases,
                            layer=j, ln_row=2 * i + 1)
        else:
            x2 = x.reshape(t, d)
            qkv = _project(x2, ret_w_in, cos, sin, layer=j, col_start=0, n_cols=2 * qk_dim + v_dim,
                           rope_cols=2 * qk_dim, head_dim=head_dim, out_dtype=BF16)
            o = _retention_core(qkv, x2, ret_w_in, layer=j, batch=batch)
            x1, x1_packed, idx, gates = _outproj_ln_router(o, ret_w_o, x2, gains, biases, moe_w_router,
                                                           layer=j, ln_row=2 * i)
            x2 = _moe_ln(x1, x1_packed, idx, gates, moe_w_gate, moe_w_up, moe_w_down, gains, biases,
                         layer=j, ln_row=2 * i + 1)
        x = x2.reshape(batch, seq, d)
    return x
```

```python
import functools
import math

import jax
import jax.numpy as jnp
from jax import lax
from jax.experimental import pallas as pl
from jax.experimental.pallas import tpu as pltpu

F32 = jnp.float32
BF16 = jnp.bfloat16

DEPTH = 4
POOL_WINDOWS = (2, 4, 8, 16)
RET_HEADS = 8
CHUNK = 64
ROPE_BASE = 10000.0
N_EXPERTS = 8
TOP_K = 2
LN_EPS = 1e-5
GN_EPS = 1e-6
ALPHA = (2 * DEPTH) ** 0.25
LOG_GAMMA = tuple(math.log1p(-(2.0 ** (-5 - h))) for h in range(RET_HEADS))

V7X_VMEM_BYTES = 64 * 1024 * 1024
V7X_F32_SUBLANES = 8
POOL_HALO = 2 * V7X_F32_SUBLANES

VMEM_LIMIT = V7X_VMEM_BYTES - 8 * 1024 * 1024
POOL_ROWS = 512
PROJ_ROWS, PROJ_COLS = 1024, 1024
RET_BLOCK = 256
OUTPROJ_ROWS = 512
FFN_ROWS, FFN_COLS = 512, 512
MOE_ROWS, MOE_COLS = 1024, 256
SUB_ROWS = 256
ROUTE_ROWS = 512
COMBINE_GROUPS = 16


def _params(*semantics):
    return pltpu.CompilerParams(dimension_semantics=semantics, vmem_limit_bytes=VMEM_LIMIT)


def _layer_norm(z, gain, bias):
    mu = jnp.mean(z, axis=-1, keepdims=True)
    zc = z - mu
    var = jnp.mean(zc * zc, axis=-1, keepdims=True)
    return zc * lax.rsqrt(var + LN_EPS) * gain + bias


def _row(ref, r):
    return ref[r:r + 1, :]


def _pool_ln_kernel(x_ref, halo_ref, w_ref, scale_ref, gain_ref, bias_ref, o_ref, *, layer, ln_row):
    ts, d = x_ref.shape[1], x_ref.shape[2]
    group = d // len(POOL_WINDOWS)
    s = pl.program_id(1)
    x = x_ref[0]
    halo = jnp.where(s > 0, halo_ref[0], 0.0)
    ext = jnp.concatenate([halo, x], axis=0)
    seen = (s * ts + 1 + lax.broadcasted_iota(jnp.int32, (ts, 1), 0)).astype(F32)
    mixed = []
    for g, win in enumerate(POOL_WINDOWS):
        cols = slice(g * group, (g + 1) * group)
        acc = ext[:, cols]
        span = 1
        while span < win:
            acc = acc + pltpu.roll(acc, span, 0)
            span *= 2
        pooled = acc[POOL_HALO:] * (1.0 / jnp.minimum(seen, float(win)))
        diff = (pooled - x[:, cols]).astype(BF16)
        mixed.append(jnp.dot(diff, w_ref[g].astype(BF16), preferred_element_type=F32))
    mix = jnp.concatenate(mixed, axis=-1) * _row(scale_ref, layer)
    o_ref[0] = _layer_norm(ALPHA * x + mix, _row(gain_ref, ln_row), _row(bias_ref, ln_row))


def _pool_ln(x, pool_w, pool_scale, gains, biases, *, layer, ln_row):
    b, s, d = x.shape
    ts = min(POOL_ROWS, s)
    assert s % ts == 0 and ts % POOL_HALO == 0 and max(POOL_WINDOWS) <= POOL_HALO
    halo_blocks = ts // POOL_HALO
    n_groups, group = pool_w.shape[1], pool_w.shape[2]
    full = lambda a: pl.BlockSpec(a.shape, lambda bi, si: (0,) * a.ndim)
    return pl.pallas_call(
        functools.partial(_pool_ln_kernel, layer=layer, ln_row=ln_row),
        grid=(b, s // ts),
        in_specs=[
            pl.BlockSpec((1, ts, d), lambda bi, si: (bi, si, 0)),
            pl.BlockSpec((1, POOL_HALO, d), lambda bi, si: (bi, jnp.maximum(si * halo_blocks - 1, 0), 0)),
            pl.BlockSpec((None, n_groups, group, group), lambda bi, si: (layer, 0, 0, 0)),
            full(pool_scale), full(gains), full(biases),
        ],
        out_specs=pl.BlockSpec((1, ts, d), lambda bi, si: (bi, si, 0)),
        out_shape=jax.ShapeDtypeStruct(x.shape, F32),
        compiler_params=_params("parallel", "parallel"),
        name="pool_ln",
    )(x, x, pool_w, pool_scale, gains, biases)


def _cast_weights(w_refs, wb_refs):
    for w_ref, wb_ref in zip(w_refs, wb_refs):
        wb_ref[...] = w_ref[...].astype(BF16)


def _swiglu_rows(rows, xb_ref, wgb_ref, wub_ref, wdb_ref, acc_ref):
    xb = xb_ref[rows, :]
    gate = jnp.dot(xb, wgb_ref[...], preferred_element_type=F32)
    up = jnp.dot(xb, wub_ref[...], preferred_element_type=F32)
    hidden = (jax.nn.silu(gate) * up).astype(BF16)
    acc_ref[rows, :] += jnp.dot(hidden, wdb_ref[...], preferred_element_type=F32)


def _swiglu_ln_kernel(x_ref, wg_ref, wu_ref, wd_ref, gain_ref, bias_ref, o_ref, xb_ref, *, ln_row):
    f = pl.program_id(1)
    sub_blocks = [pl.ds(r, SUB_ROWS) for r in range(0, x_ref.shape[0], SUB_ROWS)]

    @pl.when(f == 0)
    def _():
        xb_ref[...] = x_ref[...].astype(BF16)
        o_ref[...] = jnp.zeros_like(o_ref)

    _swiglu_rows(pl.ds(0, x_ref.shape[0]), xb_ref, wg_ref, wu_ref, wd_ref, o_ref)

    @pl.when(f == pl.num_programs(1) - 1)
    def _():
        for rows in sub_blocks:
            o_ref[rows, :] = _layer_norm(ALPHA * x_ref[rows, :] + o_ref[rows, :],
                                         _row(gain_ref, ln_row), _row(bias_ref, ln_row))


def _swiglu_ln(x, w_gate, w_up, w_down, gains, biases, *, layer, ln_row):
    t, d = x.shape
    d_ff = w_gate.shape[2]
    tm, tf = min(FFN_ROWS, t), min(FFN_COLS, d_ff)
    assert t % tm == 0 and d_ff % tf == 0 and tm % SUB_ROWS == 0
    full = lambda a: pl.BlockSpec(a.shape, lambda i, f: (0,) * a.ndim)
    return pl.pallas_call(
        functools.partial(_swiglu_ln_kernel, ln_row=ln_row),
        grid=(t // tm, d_ff // tf),
        in_specs=[
            pl.BlockSpec((tm, d), lambda i, f: (i, 0)),
            pl.BlockSpec((None, d, tf), lambda i, f: (layer, 0, f)),
            pl.BlockSpec((None, d, tf), lambda i, f: (layer, 0, f)),
            pl.BlockSpec((None, tf, d), lambda i, f: (layer, f, 0)),
            full(gains), full(biases),
        ],
        out_specs=pl.BlockSpec((tm, d), lambda i, f: (i, 0)),
        out_shape=jax.ShapeDtypeStruct((t, d), F32),
        scratch_shapes=[pltpu.VMEM((tm, d), BF16)],
        compiler_params=_params("parallel", "arbitrary"),
        name="swiglu_ln",
    )(x, w_gate, w_up, w_down, gains, biases)


def _proj_kernel(x_ref, w_ref, cos_ref, sin_ref, o_ref, xb_ref, *, rope_cols, k_cols_from, head_dim):
    j = pl.program_id(1)
    tn = o_ref.shape[1]

    @pl.when(j == 0)
    def _():
        xb_ref[...] = x_ref[...].astype(BF16)

    y = jnp.dot(xb_ref[...], w_ref[...].astype(BF16), preferred_element_type=F32)

    @pl.when(j * tn < rope_cols)
    def _():
        half = head_dim // 2
        cos, sin = cos_ref[...], sin_ref[...]
        scale = jnp.where(j * tn >= k_cols_from, head_dim ** -0.5, 1.0).astype(F32)
        parts = []
        for h in range(tn // head_dim):
            a1 = y[:, h * head_dim:h * head_dim + half]
            a2 = y[:, h * head_dim + half:(h + 1) * head_dim]
            parts.append((a1 * cos - a2 * sin) * scale)
            parts.append((a1 * sin + a2 * cos) * scale)
        o_ref[...] = jnp.concatenate(parts, axis=-1).astype(o_ref.dtype)

    @pl.when(j * tn >= rope_cols)
    def _():
        o_ref[...] = y.astype(o_ref.dtype)


def _project(x, w_in, cos, sin, *, layer, col_start, n_cols, rope_cols, head_dim, out_dtype):
    t, d = x.shape
    seq = cos.shape[0]
    tm, tn = min(PROJ_ROWS, seq), PROJ_COLS
    assert t % tm == 0 and seq % tm == 0 and n_cols % tn == 0 and col_start % tn == 0
    assert tn % head_dim == 0 and rope_cols % tn == 0 and cos.shape[1] == head_dim // 2
    col0 = col_start // tn
    seq_tiles = seq // tm
    return pl.pallas_call(
        functools.partial(_proj_kernel, rope_cols=rope_cols, k_cols_from=rope_cols // 2, head_dim=head_dim),
        grid=(t // tm, n_cols // tn),
        in_specs=[
            pl.BlockSpec((tm, d), lambda i, j: (i, 0)),
            pl.BlockSpec((None, d, tn), lambda i, j: (layer, 0, col0 + j)),
            pl.BlockSpec((tm, head_dim // 2), lambda i, j: (i % seq_tiles, 0)),
            pl.BlockSpec((tm, head_dim // 2), lambda i, j: (i % seq_tiles, 0)),
        ],
        out_specs=pl.BlockSpec((tm, tn), lambda i, j: (i, j)),
        out_shape=jax.ShapeDtypeStruct((t, n_cols), out_dtype),
        scratch_shapes=[pltpu.VMEM((tm, d), BF16)],
        compiler_params=_params("parallel", "arbitrary"),
        name="ret_proj",
    )(x, w_in, cos, sin)


def _retention_kernel(q_ref, k_ref, v_ref, x_ref, wg_ref, o_ref, state_ref, decay_ref, q_decay_ref, k_decay_ref):
    blk = pl.program_id(1)
    length = q_ref.shape[1]
    dk = q_ref.shape[2] // RET_HEADS
    dv = v_ref.shape[2] // RET_HEADS

    @pl.when(blk == 0)
    def _():
        state_ref[...] = jnp.zeros_like(state_ref)
        row = lax.broadcasted_iota(jnp.int32, (length, length), 0)
        col = lax.broadcasted_iota(jnp.int32, (length, length), 1)
        dist = jnp.abs(row - col).astype(F32)
        visible = (col // CHUNK) <= (row // CHUNK)
        pos = lax.broadcasted_iota(jnp.int32, (length, dk), 0).astype(F32)
        for h in range(RET_HEADS):
            lg = LOG_GAMMA[h]
            decay_ref[h] = jnp.where(visible, jnp.exp(dist * lg), 0.0)
            q_decay_ref[h] = jnp.exp((pos + 1.0) * lg)
            k_decay_ref[h] = jnp.exp((length - 1.0 - pos) * lg)

    nt = (((1,), (1,)), ((), ()))
    tn = (((0,), (0,)), ((), ()))
    xb = x_ref[0].astype(BF16)
    for h in range(RET_HEADS):
        lg = LOG_GAMMA[h]
        q = q_ref[0, :, h * dk:(h + 1) * dk]
        k = k_ref[0, :, h * dk:(h + 1) * dk]
        v = v_ref[0, :, h * dv:(h + 1) * dv]
        scores = lax.dot_general(q, k, nt, preferred_element_type=F32) * decay_ref[h]
        q_in = (q.astype(F32) * q_decay_ref[h]).astype(BF16)
        k_out = (k.astype(F32) * k_decay_ref[h]).astype(BF16)
        state = state_ref[h]
        out = (jnp.dot(scores.astype(BF16), v, preferred_element_type=F32)
               + jnp.dot(q_in, state.astype(BF16), preferred_element_type=F32))
        state_ref[h] = math.exp(length * lg) * state + lax.dot_general(k_out, v, tn, preferred_element_type=F32)
        mu = jnp.mean(out, axis=-1, keepdims=True)
        oc = out - mu
        var = jnp.mean(oc * oc, axis=-1, keepdims=True)
        normed = oc * lax.rsqrt(var + GN_EPS)
        gate = jnp.dot(xb, wg_ref[:, h * dv:(h + 1) * dv], preferred_element_type=F32)
        o_ref[0, :, h * dv:(h + 1) * dv] = (jax.nn.silu(gate) * normed).astype(o_ref.dtype)


def _retention_core(qkv, x, w_in, *, layer, batch):
    t, d = x.shape
    v_dim = w_in.shape[2] - qkv.shape[1]
    qk_dim = (qkv.shape[1] - v_dim) // 2
    seq = t // batch
    length = min(RET_BLOCK, seq)
    assert seq % length == 0 and length % CHUNK == 0
    qkv3 = qkv.reshape(batch, seq, qkv.shape[1])
    v_block = 2 * qk_dim // v_dim
    gate_block = qkv.shape[1] // v_dim
    assert v_block * v_dim == 2 * qk_dim and gate_block * v_dim == qkv.shape[1]
    out = pl.pallas_call(
        _retention_kernel,
        grid=(batch, seq // length),
        in_specs=[
            pl.BlockSpec((1, length, qk_dim), lambda b, s: (b, s, 0)),
            pl.BlockSpec((1, length, qk_dim), lambda b, s: (b, s, 1)),
            pl.BlockSpec((1, length, v_dim), lambda b, s: (b, s, v_block)),
            pl.BlockSpec((1, length, d), lambda b, s: (b, s, 0)),
            pl.BlockSpec((None, d, v_dim), lambda b, s: (layer, 0, gate_block), pipeline_mode=pl.Buffered(1)),
        ],
        out_specs=pl.BlockSpec((1, length, v_dim), lambda b, s: (b, s, 0)),
        out_shape=jax.ShapeDtypeStruct((batch, seq, v_dim), BF16),
        scratch_shapes=[pltpu.VMEM((RET_HEADS, qk_dim // RET_HEADS, v_dim // RET_HEADS), F32),
                        pltpu.VMEM((RET_HEADS, length, length), F32),
                        pltpu.VMEM((RET_HEADS, length, qk_dim // RET_HEADS), F32),
                        pltpu.VMEM((RET_HEADS, length, qk_dim // RET_HEADS), F32)],
        compiler_params=_params("parallel", "arbitrary"),
        name="ret_core",
    )(qkv3, qkv3, qkv3, x.reshape(batch, seq, d), w_in)
    return out.reshape(t, v_dim)


def _pack_bf16_pairs(x):
    half = x.shape[1] // 2
    lo = lax.bitcast_convert_type(x[:, :half].astype(BF16).astype(F32), jnp.uint32)
    hi = lax.bitcast_convert_type(x[:, half:].astype(BF16).astype(F32), jnp.uint32)
    return hi | (lo >> 16)


def _unpack_bf16_pairs(xp):
    lo = lax.bitcast_convert_type(xp << 16, F32).astype(BF16)
    hi = lax.bitcast_convert_type(xp & jnp.uint32(0xFFFF0000), F32).astype(BF16)
    return jnp.concatenate([lo, hi], axis=1)


def _outproj_ln_router_kernel(o_ref, wo_ref, x_ref, gain_ref, bias_ref, wr_ref,
                              x1_ref, xp_ref, idx_ref, gates_ref, *, ln_row):
    w = wr_ref[...]
    w_hi = w.astype(BF16)
    w_lo = (w - w_hi.astype(F32)).astype(BF16)
    for r in range(0, x_ref.shape[0], SUB_ROWS):
        rows = pl.ds(r, SUB_ROWS)
        mix = jnp.dot(o_ref[rows, :], wo_ref[...], preferred_element_type=F32)
        x1 = _layer_norm(ALPHA * x_ref[rows, :] + mix, _row(gain_ref, ln_row), _row(bias_ref, ln_row))
        x1_ref[rows, :] = x1
        xp_ref[rows, :] = _pack_bf16_pairs(x1)
        x_hi = x1.astype(BF16)
        x_lo = (x1 - x_hi.astype(F32)).astype(BF16)
        logits = (jnp.dot(x_hi, w_hi, preferred_element_type=F32)
                  + (jnp.dot(x_hi, w_lo, preferred_element_type=F32)
                     + jnp.dot(x_lo, w_hi, preferred_element_type=F32)))
        expert = lax.broadcasted_iota(jnp.int32, logits.shape, 1).astype(F32)
        none = float(N_EXPERTS)
        m1 = jnp.max(logits, axis=1, keepdims=True)
        i1 = jnp.min(jnp.where(logits == m1, expert, none), axis=1, keepdims=True)
        rest = jnp.where(expert == i1, -jnp.inf, logits)
        m2 = jnp.max(rest, axis=1, keepdims=True)
        i2 = jnp.min(jnp.where(rest == m2, expert, none), axis=1, keepdims=True)
        p2 = jnp.exp(m2 - m1)
        denom = 1.0 + p2
        first = lax.broadcasted_iota(jnp.int32, (SUB_ROWS, TOP_K), 1) == 0
        idx_ref[rows, :] = jnp.where(first, i1, i2).astype(jnp.int32)
        gates_ref[rows, :] = jnp.where(first, 1.0 / denom, p2 / denom)


def _outproj_ln_router(o, w_o, x, gains, biases, w_router, *, layer, ln_row):
    t, d = x.shape
    v_dim = o.shape[1]
    tm = min(OUTPROJ_ROWS, t)
    assert t % tm == 0 and tm % SUB_ROWS == 0 and TOP_K == 2 and w_o.dtype == BF16
    full = lambda a: pl.BlockSpec(a.shape, lambda i: (0,) * a.ndim)
    return pl.pallas_call(
        functools.partial(_outproj_ln_router_kernel, ln_row=ln_row),
        grid=(t // tm,),
        in_specs=[
            pl.BlockSpec((tm, v_dim), lambda i: (i, 0)),
            pl.BlockSpec((None, v_dim, d), lambda i: (layer, 0, 0), pipeline_mode=pl.Buffered(1)),
            pl.BlockSpec((tm, d), lambda i: (i, 0)),
            full(gains), full(biases),
            pl.BlockSpec((None, d, N_EXPERTS), lambda i: (layer, 0, 0)),
        ],
        out_specs=[
            pl.BlockSpec((tm, d), lambda i: (i, 0)),
            pl.BlockSpec((tm, d // 2), lambda i: (i, 0)),
            pl.BlockSpec((tm, TOP_K), lambda i: (i, 0)),
            pl.BlockSpec((tm, TOP_K), lambda i: (i, 0)),
        ],
        out_shape=[
            jax.ShapeDtypeStruct((t, d), F32),
            jax.ShapeDtypeStruct((t, d // 2), jnp.uint32),
            jax.ShapeDtypeStruct((t, TOP_K), jnp.int32),
            jax.ShapeDtypeStruct((t, TOP_K), F32),
        ],
        compiler_params=_params("parallel"),
        name="outproj_ln_router",
    )(o, w_o, x, gains, biases, w_router)


def _route_plan(idx, block_rows):
    flat_e = idx.reshape(-1)
    n_assign = flat_e.shape[0]
    experts = jnp.arange(N_EXPERTS, dtype=jnp.int32)
    onehot = (flat_e[:, None] == experts[None, :]).astype(jnp.int32)
    running = jnp.cumsum(onehot, axis=0)
    rank = jnp.sum((running - onehot) * onehot, axis=1)
    counts = running[-1]
    padded = (counts + block_rows - 1) // block_rows * block_rows
    pad_end = jnp.cumsum(padded)
    pad_start = pad_end - padded
    dest = (jnp.sum(onehot * pad_start[None, :], axis=1) + rank).astype(jnp.int32)
    n_blocks = n_assign // block_rows + N_EXPERTS
    block_start = jnp.arange(n_blocks, dtype=jnp.int32) * block_rows
    block_expert = jnp.minimum(jnp.sum(block_start[:, None] >= pad_end[None, :], axis=1), N_EXPERTS - 1)
    live_rows = jnp.clip((pad_start + counts)[block_expert] - block_start, 0, block_rows)
    live_sub = (live_rows + SUB_ROWS - 1) // SUB_ROWS
    return dest, block_expert.astype(jnp.int32), live_sub.astype(jnp.int32), n_blocks


def _for_each_sublane_group(n_groups, body):
    lax.fori_loop(0, n_groups, lambda g, carry: (body(g), carry)[1], 0)


def _dispatch_kernel(dest_ref, x_ref, init_hbm, xs_hbm, sem):
    del init_hbm
    groups, sub = x_ref.shape[0], x_ref.shape[1]
    first = pl.program_id(0) * (groups * sub * TOP_K)

    def row_copies(g):
        at = first + g * (sub * TOP_K)
        return [pltpu.make_async_copy(x_ref.at[g, pl.ds(u, 1)], xs_hbm.at[pl.ds(dest_ref[at + TOP_K * u + k], 1)], sem)
                for u in range(sub) for k in range(TOP_K)]

    _for_each_sublane_group(groups, lambda g: [c.start(priority=n % 2) for n, c in enumerate(row_copies(g))])
    _for_each_sublane_group(groups, lambda g: [c.wait() for c in row_copies(g)])


def _dispatch(x, dest, n_rows):
    t, d = x.shape
    tokens = min(ROUTE_ROWS, t)
    sub = V7X_F32_SUBLANES
    assert t % tokens == 0 and tokens % sub == 0
    return pl.pallas_call(
        _dispatch_kernel,
        grid_spec=pltpu.PrefetchScalarGridSpec(
            num_scalar_prefetch=1,
            grid=(t // tokens,),
            in_specs=[pl.BlockSpec((tokens // sub, sub, d), lambda i, dest: (i, 0, 0)),
                      pl.BlockSpec(memory_space=pl.ANY)],
            out_specs=pl.BlockSpec(memory_space=pl.ANY),
            scratch_shapes=[pltpu.SemaphoreType.DMA(())],
        ),
        out_shape=jax.ShapeDtypeStruct((n_rows, d), x.dtype),
        input_output_aliases={2: 0},
        compiler_params=pltpu.CompilerParams(dimension_semantics=("arbitrary",), has_side_effects=True),
        name="moe_dispatch",
    )(dest, x.reshape(t // sub, sub, d), jnp.zeros((n_rows, d), x.dtype))


def _expert_kernel(block_expert_ref, live_sub_ref, xs_ref, wg_ref, wu_ref, wd_ref, ys_ref,
                   xb_ref, wgb_ref, wub_ref, wdb_ref):
    del block_expert_ref
    i, f = pl.program_id(0), pl.program_id(1)
    live_sub = live_sub_ref[i]

    @pl.when(f == 0)
    def _():
        xb_ref[...] = _unpack_bf16_pairs(xs_ref[...])
        ys_ref[...] = jnp.zeros_like(ys_ref)

    n_sub = xs_ref.shape[0] // SUB_ROWS
    sub_blocks = [pl.ds(sb * SUB_ROWS, SUB_ROWS) for sb in range(n_sub)]

    @pl.when(live_sub == n_sub)
    def _():
        _cast_weights((wg_ref, wu_ref, wd_ref), (wgb_ref, wub_ref, wdb_ref))
        _swiglu_rows(pl.ds(0, xs_ref.shape[0]), xb_ref, wgb_ref, wub_ref, wdb_ref, ys_ref)

    @pl.when(jnp.logical_and(live_sub > 0, live_sub < n_sub))
    def _():
        _cast_weights((wg_ref, wu_ref, wd_ref), (wgb_ref, wub_ref, wdb_ref))
        for sb in range(n_sub - 1):
            @pl.when(sb < live_sub)
            def _():
                _swiglu_rows(sub_blocks[sb], xb_ref, wgb_ref, wub_ref, wdb_ref, ys_ref)


def _experts(xs, block_expert, live_sub, w_gate, w_up, w_down, *, layer, block_rows):
    n_rows = xs.shape[0]
    d, d_ff = w_gate.shape[2], w_gate.shape[3]
    tf = min(MOE_COLS, d_ff)
    assert n_rows % block_rows == 0 and d_ff % tf == 0 and block_rows % SUB_ROWS == 0 and xs.shape[1] * 2 == d
    n_f = d_ff // tf

    def col(i, f, live):
        return jnp.where(live[i] > 0, f, n_f - 1)

    return pl.pallas_call(
        _expert_kernel,
        grid_spec=pltpu.PrefetchScalarGridSpec(
            num_scalar_prefetch=2,
            grid=(n_rows // block_rows, n_f),
            in_specs=[
                pl.BlockSpec((block_rows, d // 2), lambda i, f, be, live: (i, 0)),
                pl.BlockSpec((None, None, d, tf), lambda i, f, be, live: (layer, be[i], 0, col(i, f, live))),
                pl.BlockSpec((None, None, d, tf), lambda i, f, be, live: (layer, be[i], 0, col(i, f, live))),
                pl.BlockSpec((None, None, tf, d), lambda i, f, be, live: (layer, be[i], col(i, f, live), 0)),
            ],
            out_specs=pl.BlockSpec((block_rows, d), lambda i, f, be, live: (i, 0)),
            scratch_shapes=[pltpu.VMEM((block_rows, d), BF16), pltpu.VMEM((d, tf), BF16),
                            pltpu.VMEM((d, tf), BF16), pltpu.VMEM((tf, d), BF16)],
        ),
        out_shape=jax.ShapeDtypeStruct((n_rows, d), F32),
        compiler_params=_params("parallel", "arbitrary"),
        name="moe_experts",
    )(block_expert, live_sub, xs, w_gate, w_up, w_down)


def _combine_ln_kernel(dest_ref, x_ref, gates_ref, gain_ref, bias_ref, ys_hbm, o_ref, y_even, y_odd, sems, *, ln_row):
    groups, sub = x_ref.shape[0], x_ref.shape[1]
    tile = pl.program_id(0)
    per_tile = groups * sub * TOP_K
    bufs = (y_even, y_odd)

    def row_copies(which, parity, g):
        at = which * per_tile + g * (sub * TOP_K)
        return [pltpu.make_async_copy(ys_hbm.at[pl.ds(dest_ref[at + TOP_K * u + k], 1)],
                                      bufs[parity].at[k, g, pl.ds(u, 1)], sems.at[parity])
                for u in range(sub) for k in range(TOP_K)]

    def normalise(parity, g0):
        rows = pl.ds(g0, COMBINE_GROUPS)
        gates = gates_ref[rows]
        mix = gates[:, :, 0:1] * bufs[parity][0, rows] + gates[:, :, 1:2] * bufs[parity][1, rows]
        o_ref[rows] = _layer_norm(ALPHA * x_ref[rows] + mix, _row(gain_ref, ln_row), _row(bias_ref, ln_row))

    @pl.when(tile == 0)
    def _():
        _for_each_sublane_group(groups, lambda g: [c.start() for c in row_copies(tile, 0, g)])

    for parity in range(2):
        @pl.when((tile & 1) == parity)
        def _():
            _for_each_sublane_group(groups, lambda g: [c.wait() for c in row_copies(tile, parity, g)])

            @pl.when(tile + 1 < pl.num_programs(0))
            def _():
                def step(i):
                    g0 = pl.multiple_of(i * COMBINE_GROUPS, COMBINE_GROUPS)
                    for g in range(COMBINE_GROUPS):
                        for c in row_copies(tile + 1, 1 - parity, g0 + g):
                            c.start()
                    normalise(parity, g0)
                _for_each_sublane_group(groups // COMBINE_GROUPS, step)

            @pl.when(tile + 1 == pl.num_programs(0))
            def _():
                _for_each_sublane_group(
                    groups // COMBINE_GROUPS,
                    lambda i: normalise(parity, pl.multiple_of(i * COMBINE_GROUPS, COMBINE_GROUPS)))


def _combine_ln(x, ys, dest, gates, gains, biases, *, ln_row):
    t, d = x.shape
    tokens = min(ROUTE_ROWS, t)
    sub = V7X_F32_SUBLANES
    assert t % tokens == 0 and tokens % (sub * COMBINE_GROUPS) == 0
    groups = tokens // sub
    full = lambda a: pl.BlockSpec(a.shape, lambda i, dest: (0,) * a.ndim)
    out = pl.pallas_call(
        functools.partial(_combine_ln_kernel, ln_row=ln_row),
        grid_spec=pltpu.PrefetchScalarGridSpec(
            num_scalar_prefetch=1,
            grid=(t // tokens,),
            in_specs=[
                pl.BlockSpec((groups, sub, d), lambda i, dest: (i, 0, 0)),
                pl.BlockSpec((groups, sub, TOP_K), lambda i, dest: (i, 0, 0)),
                full(gains), full(biases),
                pl.BlockSpec(memory_space=pl.ANY),
            ],
            out_specs=pl.BlockSpec((groups, sub, d), lambda i, dest: (i, 0, 0)),
            scratch_shapes=[pltpu.VMEM((TOP_K, groups, sub, d), F32), pltpu.VMEM((TOP_K, groups, sub, d), F32),
                            pltpu.SemaphoreType.DMA((2,))],
        ),
        out_shape=jax.ShapeDtypeStruct((t // sub, sub, d), F32),
        compiler_params=_params("arbitrary"),
        name="moe_combine_ln",
    )(dest, x.reshape(t // sub, sub, d), gates.reshape(t // sub, sub, TOP_K), gains, biases, ys)
    return out.reshape(t, d)


def _moe_ln(x, x_packed, idx, gates, w_gate, w_up, w_down, gains, biases, *, layer, ln_row):
    dest, block_expert, live_sub, n_blocks = _route_plan(idx, MOE_ROWS)
    xs = _dispatch(x_packed, dest, n_blocks * MOE_ROWS)
    ys = _experts(xs, block_expert, live_sub, w_gate, w_up, w_down, layer=layer, block_rows=MOE_ROWS)
    return _combine_ln(x, ys, dest, gates, gains, biases, ln_row=ln_row)


def _rope_tables(seq, head_dim):
    half = head_dim // 2
    inv_freq = ROPE_BASE ** (-jnp.arange(half, dtype=F32) / half)
    ang = jnp.arange(seq, dtype=F32)[:, None] * inv_freq[None, :]
    return jnp.cos(ang), jnp.sin(ang)


def kernel(x, ln_gain, ln_bias, pool_w, pool_scale, ret_w_in, ret_w_o, ffn_w_gate, ffn_w_up, ffn_w_down,
           moe_w_router, moe_w_gate, moe_w_up, moe_w_down):
    batch, seq, d = x.shape
    t = batch * seq
    gains = ln_gain.reshape(2 * DEPTH, d)
    biases = ln_bias.reshape(2 * DEPTH, d)
    qk_dim = ret_w_o.shape[2]
    v_dim = ret_w_o.shape[1]
    head_dim = qk_dim // RET_HEADS
    cos, sin = _rope_tables(seq, head_dim)
    pool_w = pool_w.astype(BF16)
    ret_w_in = ret_w_in.astype(BF16)
    ret_w_o = ret_w_o.astype(BF16)
    ffn_w_gate, ffn_w_up, ffn_w_down = (w.astype(BF16) for w in (ffn_w_gate, ffn_w_up, ffn_w_down))

    for i in range(DEPTH):
        j = i // 2
        if i % 2 == 0:
            x = _pool_ln(x, pool_w, pool_scale, gains, biases, layer=j, ln_row=2 * i)
            x2 = _swiglu_ln(x.reshape(t, d), ffn_w_gate, ffn_w_up, ffn_w_down, gains, biases,
                            layer=j, ln_row=2 * i + 1)
        else:
            x2 = x.reshape(t, d)
            qkv = _project(x2, ret_w_in, cos, sin, layer=j, col_start=0, n_cols=2 * qk_dim + v_dim,
                           rope_cols=2 * qk_dim, head_dim=head_dim, out_dtype=BF16)
            o = _retention_core(qkv, x2, ret_w_in, layer=j, batch=batch)
            x1, x1_packed, idx, gates = _outproj_ln_router(o, ret_w_o, x2, gains, biases, moe_w_router,
                                                           layer=j, ln_row=2 * i)
            x2 = _moe_ln(x1, x1_packed, idx, gates, moe_w_gate, moe_w_up, moe_w_down, gains, biases,
                         layer=j, ln_row=2 * i + 1)
        x = x2.reshape(batch, seq, d)
    return x
```
